```python
import math
import jax, jax.numpy as jnp
from jax import lax
import numpy as np


D_MODEL = 1024
BATCH = 4
SEQ = 4096
DEPTH = 2

CTX_LEN = 256
GRID_W = 64

HA = 8
DH_A = 64
DV_A = 2 * DH_A
D_A = HA * DV_A
ROPE_BASE = 10000.0
Q_BLOCK = 128
F_GROUPS = 4
F_GROUP_W = 128
D_F = F_GROUPS * F_GROUP_W
HC = 8
HS_C = 64
D_C = HC * HS_C
LORA_W = 64
LORA_A = 64
LORA_G = 128
D_CS = 2 * D_C + 2 * LORA_W + 2 * LORA_A
D_CO = D_C + LORA_G
N_BRANCH = 3
D_FF = 2816
CONV_W = 3
LN_EPS = 1e-5
GN_EPS = HS_C * 1e-5
HEAD_NORM_EPS = 1e-5

IN_GROUPS = (('q', HA * 2 * DH_A), ('k', HA * 2 * DH_A), ('v', D_A), ('f', D_F),
             ('cs', D_CS), ('co', D_CO), ('gate', N_BRANCH * D_MODEL))
D_IN = 2 * HA * 2 * DH_A + D_A + D_F + D_CS + D_CO + N_BRANCH * D_MODEL

kernel_name = 'hybrid_diffattn_fnet_rwkv7_prefix_block'


def _ln_plain(x):
    xf = x.astype(jnp.float32)
    mu = jnp.mean(xf, axis=-1, keepdims=True)
    var = jnp.mean(jnp.square(xf - mu), axis=-1, keepdims=True)
    return ((xf - mu) * lax.rsqrt(var + LN_EPS)).astype(x.dtype)


def _layernorm(x, g, b):
    return _ln_plain(x) * g + b


def _modulate(x, shift, scale):
    return _ln_plain(x) * (1.0 + scale) + shift


def _in_proj(h, w_in, names=None):
    out, off = {}, 0
    for name, width in IN_GROUPS:
        if names is None or name in names:
            out[name] = h @ w_in[:, off:off + width]
        off += width
    return out


def _shift_prev(x):
    return jnp.pad(x, ((0, 0), (1, 0), (0, 0)))[:, :-1]


def _shift_next(x):
    return jnp.pad(x, ((0, 0), (0, 1), (0, 0)))[:, 1:]


def _token_shift(p, mu):
    return p + mu[0] * (_shift_prev(p) - p) + mu[1] * (_shift_next(p) - p)


def _dwconv3(x, w, b):
    return w[0] * _shift_prev(x) + w[1] * x + w[2] * _shift_next(x) + b


def _rope_angles(n_tokens):
    rows = n_tokens // GRID_W
    row = jnp.repeat(jnp.arange(rows), GRID_W).astype(jnp.float32)
    col = jnp.tile(jnp.arange(GRID_W), rows).astype(jnp.float32)
    n_freq = DH_A // 4
    inv = ROPE_BASE ** (-jnp.arange(n_freq, dtype=jnp.float32) / n_freq)
    ang = jnp.stack([row[:, None] * inv, col[:, None] * inv], axis=1)
    return jnp.cos(ang), jnp.sin(ang)


def _apply_rope(x, cos, sin):
    sh = x.shape
    xr = x.reshape(sh[:-1] + (2, 2, DH_A // 4))
    x1, x2 = xr[..., 0, :], xr[..., 1, :]
    c = cos[None, :, None, None].astype(x.dtype)
    s = sin[None, :, None, None].astype(x.dtype)
    return jnp.stack([x1 * c - x2 * s, x2 * c + x1 * s], axis=-2).reshape(sh)


def _diff_attend(parts, lam):
    scale = DH_A ** -0.5
    s = jnp.concatenate([jnp.einsum('bqhcd,bkhcd->bhcqk', q, k).astype(jnp.float32) for q, k, _ in parts], axis=-1) * scale
    p = jax.nn.softmax(s, axis=-1)
    a = (p[:, :, 0] - lam * p[:, :, 1]).astype(parts[0][2].dtype)
    out, off = None, 0
    for _, k, v in parts:
        n = k.shape[1]
        t = jnp.einsum('bhqk,bkhe->bqhe', a[..., off:off + n], v)
        out = t if out is None else out + t
        off += n
    return out


def _latent_diff_attn(q, k, v, kc, vc, cos, sin, lam):
    B, T = q.shape[:2]
    qr = _apply_rope(q, cos, sin)
    kr = _apply_rope(k, cos, sin)
    nb = T // Q_BLOCK

    def to_blocks(t):
        return jnp.moveaxis(t.reshape((B, nb, Q_BLOCK) + t.shape[2:]), 1, 0)

    def blk(qs):
        q_rot, q_plain = qs
        return _diff_attend([(q_rot, kr, v), (q_plain, kc, vc)], lam)

    o = lax.map(blk, (to_blocks(qr), to_blocks(q)))
    return jnp.moveaxis(o, 0, 1).reshape((B, T) + o.shape[3:])


def _diff_out(o, subln_g, lam_init):
    B, T = o.shape[:2]
    of = o.astype(jnp.float32)
    of = of * lax.rsqrt(jnp.mean(jnp.square(of), axis=-1, keepdims=True) + HEAD_NORM_EPS) * subln_g * (1.0 - lam_init)
    return of.reshape(B, T, D_A).astype(o.dtype)


def _fourier(pf):
    B, T, _ = pf.shape
    z = jnp.fft.fftn(pf.astype(jnp.float32).reshape(B, T, F_GROUPS, F_GROUP_W), axes=(1, 3), norm='ortho')
    return z.real.reshape(B, T, D_F).astype(pf.dtype)


def _heads(t):
    return t.reshape(t.shape[:-1] + (HC, HS_C))


def _l2norm_heads(t):
    th = _heads(t).astype(jnp.float32)
    th = th / jnp.maximum(jnp.sqrt(jnp.sum(jnp.square(th), axis=-1, keepdims=True)), 1e-12)
    return th.reshape(t.shape).astype(t.dtype)


def _two(t):
    return jnp.broadcast_to(t, (2,) + t.shape)


def _dir_stack(t):
    return _heads(jnp.stack([t[0], jnp.flip(t[1], axis=1)]))


def _wkv_scan(decay, kd, v, kk, b, s0, r=None):
    xs = [decay, kd, v, kk, b] + ([r] if r is not None else [])
    xs = tuple(jnp.moveaxis(t.astype(jnp.float32), 2, 0) for t in xs)

    def step(s, inp):
        w_t, k_t, v_t, kk_t, b_t = inp[:5]
        sk = jnp.einsum('dbhvk,dbhk->dbhv', s, kk_t)
        s = s * w_t[..., None, :] - sk[..., None] * b_t[..., None, :] + v_t[..., None] * k_t[..., None, :]
        y = jnp.einsum('dbhvk,dbhk->dbhv', s, inp[5]) if len(inp) == 6 else None
        return s, y

    s_fin, y = lax.scan(step, s0.astype(jnp.float32), xs)
    return s_fin, (None if y is None else jnp.moveaxis(y, 0, 2))


def _rwkv_state_inputs(cs, lp):
    B, T, _ = cs.shape
    k, v, wd, ad = jnp.split(cs, [D_C, 2 * D_C, 2 * D_C + 2 * LORA_W], axis=-1)
    wd = wd.reshape(B, T, 2, LORA_W)
    ad = ad.reshape(B, T, 2, LORA_A)
    wlog = -jax.nn.softplus(-(lp['w0'][:, None, None, :] + jnp.einsum('btdr,drc->dbtc', jnp.tanh(wd), lp['w2'])).astype(jnp.float32)) - 0.5
    decay = jnp.exp(-jnp.exp(wlog))
    a = jax.nn.sigmoid(lp['a0'][:, None, None, :] + jnp.einsum('btdr,drc->dbtc', ad, lp['a2']))
    kk = _l2norm_heads(k * lp['k_k'])
    kd = k * (1.0 + (a - 1.0) * lp['k_a'])
    b = kk * a
    return v, decay, kd, kk, b


def _rwkv_branch(cs, co, s0, lp, want_out):
    B, T, _ = cs.shape
    v, decay, kd, kk, b = _rwkv_state_inputs(cs, lp)
    scan_in = [_dir_stack(t) for t in (decay, kd, _two(v), _two(kk), b)]
    if not want_out:
        s_fin, _ = _wkv_scan(*scan_in, s0)
        return s_fin, None
    r, gd = jnp.split(co, [D_C], axis=-1)
    s_fin, y = _wkv_scan(*scan_in, s0, r=_dir_stack(_two(r)))
    y = y[0] + jnp.flip(y[1], axis=1)
    mu = jnp.mean(y, axis=-1, keepdims=True)
    var = jnp.mean(jnp.square(y - mu), axis=-1, keepdims=True)
    yn = (y - mu) * lax.rsqrt(var + GN_EPS) * _heads(lp['ln_w']) + _heads(lp['ln_b'])
    bonus = jnp.sum(_heads(r)[None] * _heads(kd) * lp['r_k'], axis=(0, -1))[..., None] * _heads(v)
    g = jax.nn.sigmoid(gd) @ lp['g2']
    return s_fin, (yn.astype(v.dtype) + bonus).reshape(B, T, D_C) * g


def _merge(gate_pre, o_a, o_f, o_c, lp):
    g = jax.nn.sigmoid(gate_pre).reshape(gate_pre.shape[:-1] + (N_BRANCH, D_MODEL))
    y = g[..., 0, :] * (o_a @ lp['w_pa']) + g[..., 1, :] * (o_f @ lp['w_pf']) + g[..., 2, :] * (o_c @ lp['w_pc'])
    return y @ lp['w_o']


def _ctx_mixer(hc, lp, lam, lam_init, want_out):
    B, L, _ = hc.shape
    P = _in_proj(hc, lp['w_in'], None if want_out else ('k', 'v', 'cs'))
    k = P['k'].reshape(B, L, HA, 2, DH_A)
    v = P['v'].reshape(B, L, HA, DV_A)
    cs = _token_shift(P['cs'], lp['mu'][:, :D_CS])
    s0 = jnp.zeros((2, B, HC, HS_C, HS_C), jnp.float32)
    if not want_out:
        s_fin, _ = _rwkv_branch(cs, None, s0, lp, False)
        return (k, v, s_fin), None
    co = _token_shift(P['co'], lp['mu'][:, D_CS:])
    s_fin, o_c = _rwkv_branch(cs, co, s0, lp, True)
    q = P['q'].reshape(B, L, HA, 2, DH_A)
    o_a = _diff_out(_diff_attend([(q, k, v)], lam), lp['subln'], lam_init)
    o_f = _fourier(P['f'])
    return (k, v, s_fin), _merge(P['gate'], o_a, o_f, o_c, lp)


def _latent_mixer(h, lp, lam, lam_init, cache, cos, sin):
    kc, vc, s_ctx = cache
    B, T, _ = h.shape
    P = _in_proj(h, lp['w_in'])
    q = P['q'].reshape(B, T, HA, 2, DH_A)
    k = P['k'].reshape(B, T, HA, 2, DH_A)
    v = P['v'].reshape(B, T, HA, DV_A)
    o_a = _diff_out(_latent_diff_attn(q, k, v, kc, vc, cos, sin, lam), lp['subln'], lam_init)
    o_f = _fourier(P['f'])
    cs = _token_shift(P['cs'], lp['mu'][:, :D_CS])
    co = _token_shift(P['co'], lp['mu'][:, D_CS:])
    _, o_c = _rwkv_branch(cs, co, s_ctx, lp, True)
    return _merge(P['gate'], o_a, o_f, o_c, lp)


def _conv_ffn(h, lp):
    u, g = jnp.split(h @ lp['ffn_up'], 2, axis=-1)
    g = _dwconv3(g, lp['ffn_conv'], lp['ffn_conv_b'])
    return (jax.nn.gelu(g, approximate=False) * u) @ lp['ffn_down']


def setup_inputs(seed: int = 0) -> dict:
    key = jax.random.key(seed)
    ks = iter(jax.random.split(key, 40))
    f32 = jnp.float32
    L, D = DEPTH, D_MODEL
    beta = (8.0 * DEPTH) ** -0.25

    def nrm(shape, scale):
        return scale * jax.random.normal(next(ks), shape, f32)

    def unif(shape, lo, hi):
        return jax.random.uniform(next(ks), shape, f32, lo, hi)

    return {
        'x': nrm((BATCH, SEQ, D), 1.0),
        'c': nrm((BATCH, D), 1.0),
        'ctx': nrm((BATCH, CTX_LEN, D), 1.0),
        'c_ctx': nrm((D,), 1.0),
        'ada_w': nrm((L, D, 6 * D), 0.5 * D ** -0.5),
        'ada_b': nrm((L, 6 * D), 0.01),
        'w_in': nrm((L, D, D_IN), D ** -0.5),
        'lam_qk': nrm((L, 4, DH_A), 0.1),
        'subln_g': 1.0 + nrm((L, DV_A), 0.02),
        'tshift_mu': unif((L, 2, D_CS + D_CO), 0.0, 0.5),
        'rw_w0': unif((L, 2, D_C), -5.0, 0.0),
        'rw_w2': nrm((L, 2, LORA_W, D_C), 0.5 * LORA_W ** -0.5),
        'rw_a0': nrm((L, 2, D_C), 0.5),
        'rw_a2': nrm((L, 2, LORA_A, D_C), 0.5 * LORA_A ** -0.5),
        'rw_g2': nrm((L, LORA_G, D_C), LORA_G ** -0.5),
        'rw_kk': 1.0 + nrm((L, D_C), 0.1),
        'rw_ka': 1.0 + nrm((L, D_C), 0.1),
        'rw_rk': nrm((L, HC, HS_C), 0.1),
        'rw_lnw': 1.0 + nrm((L, D_C), 0.02),
        'rw_lnb': nrm((L, D_C), 0.02),
        'w_pa': nrm((L, D_A, D), D_A ** -0.5),
        'w_pf': nrm((L, D_F, D), D_F ** -0.5),
        'w_pc': nrm((L, D_C, D), D_C ** -0.5),
        'w_o': nrm((L, D, D), beta * D ** -0.5),
        'ln1_g': 1.0 + nrm((L, D), 0.02),
        'ln1_b': nrm((L, D), 0.02),
        'ffn_up': nrm((L, D, 2 * D_FF), D ** -0.5),
        'ffn_conv': nrm((L, CONV_W, D_FF), CONV_W ** -0.5),
        'ffn_conv_b': nrm((L, D_FF), 0.02),
        'ffn_down': nrm((L, D_FF, D), beta * D_FF ** -0.5),
        'ln2_g': 1.0 + nrm((L, D), 0.02),
        'ln2_b': nrm((L, D), 0.02),
    }


def reference(x, c, ctx, c_ctx, ada_w, ada_b, w_in, lam_qk, subln_g, tshift_mu,
              rw_w0, rw_w2, rw_a0, rw_a2, rw_g2, rw_kk, rw_ka, rw_rk, rw_lnw, rw_lnb,
              w_pa, w_pf, w_pc, w_o, ln1_g, ln1_b, ffn_up, ffn_conv, ffn_conv_b, ffn_down,
              ln2_g, ln2_b):
    alpha = (2.0 * DEPTH) ** 0.25
    cos, sin = _rope_angles(x.shape[1])
    xc = ctx
    for i in range(DEPTH):
        last = i == DEPTH - 1
        lam_init = 0.8 - 0.6 * math.exp(-0.3 * i)
        lq = lam_qk[i].astype(jnp.float32)
        lam = jnp.exp(jnp.sum(lq[0] * lq[1])) - jnp.exp(jnp.sum(lq[2] * lq[3])) + lam_init
        lp = {'w_in': w_in[i], 'subln': subln_g[i], 'mu': tshift_mu[i],
              'w0': rw_w0[i], 'w2': rw_w2[i], 'a0': rw_a0[i], 'a2': rw_a2[i], 'g2': rw_g2[i],
              'k_k': rw_kk[i], 'k_a': rw_ka[i], 'r_k': rw_rk[i], 'ln_w': rw_lnw[i], 'ln_b': rw_lnb[i],
              'w_pa': w_pa[i], 'w_pf': w_pf[i], 'w_pc': w_pc[i], 'w_o': w_o[i],
              'ffn_up': ffn_up[i], 'ffn_conv': ffn_conv[i], 'ffn_conv_b': ffn_conv_b[i], 'ffn_down': ffn_down[i]}
        mod = [m[:, None, :] for m in jnp.split(jax.nn.silu(c) @ ada_w[i] + ada_b[i], 6, axis=-1)]
        mod_c = jnp.split(jax.nn.silu(c_ctx) @ ada_w[i] + ada_b[i], 6, axis=-1)
        cache, ctx_mix = _ctx_mixer(_modulate(xc, mod_c[0], mod_c[1]), lp, lam, lam_init, not last)
        lat_mix = _latent_mixer(_modulate(x, mod[0], mod[1]), lp, lam, lam_init, cache, cos, sin)
        x = _layernorm(alpha * x + mod[2] * lat_mix, ln1_g[i], ln1_b[i])
        x = _layernorm(alpha * x + mod[5] * _conv_ffn(_modulate(x, mod[3], mod[4]), lp), ln2_g[i], ln2_b[i])
        if not last:
            xc = _layernorm(alpha * xc + mod_c[2] * ctx_mix, ln1_g[i], ln1_b[i])
            xc = _layernorm(alpha * xc + mod_c[5] * _conv_ffn(_modulate(xc, mod_c[3], mod_c[4]), lp), ln2_g[i], ln2_b[i])
    return x
```

```python
import functools
import math

import jax
import jax.numpy as jnp
from jax import lax
from jax.experimental import pallas as pl
from jax.experimental.pallas import tpu as pltpu

F32 = jnp.float32
BF16 = jnp.bfloat16

D_MODEL = 1024
DEPTH = 2
GRID_W = 64
HA = 8
DH_A = 64
DV_A = 2 * DH_A
D_A = HA * DV_A
ROPE_BASE = 10000.0
F_GROUPS = 4
F_GROUP_W = 128
D_F = F_GROUPS * F_GROUP_W
HC = 8
HS_C = 64
D_C = HC * HS_C
LORA_W = 64
LORA_A = 64
LORA_G = 128
D_CS = 2 * D_C + 2 * LORA_W + 2 * LORA_A
D_CO = D_C + LORA_G
N_BRANCH = 3
D_FF = 2816
LN_EPS = 1e-5
GN_EPS = HS_C * 1e-5
HEAD_NORM_EPS = 1e-5

OFF_Q = 0
OFF_K = OFF_Q + D_A
OFF_V = OFF_K + D_A
OFF_F = OFF_V + D_A
OFF_CS = OFF_F + D_F
OFF_CO = OFF_CS + D_CS
OFF_GATE = OFF_CO + D_CO
D_IN = OFF_GATE + N_BRANCH * D_MODEL

LANES = 128
HALO = 16
TM = 256
CHUNK = 64
PAIR = 2 * HS_C
N_PAIR = D_C // PAIR
VMEM_LIMIT = 56 * 1024 * 1024


def _cparams(sem):
    return pltpu.CompilerParams(dimension_semantics=sem, vmem_limit_bytes=VMEM_LIMIT)


def _split2(x):
    hi = x.astype(BF16)
    lo = (x - hi.astype(F32)).astype(BF16)
    return hi, lo


def _dot(a, b, dims=(((1,), (0,)), ((), ()))):
    return lax.dot_general(a, b, dims, preferred_element_type=F32)


def _dot_nt(a, b):
    return lax.dot_general(a, b, (((1,), (1,)), ((), ())), preferred_element_type=F32)


def _dot_tn(a, b):
    return lax.dot_general(a, b, (((0,), (0,)), ((), ())), preferred_element_type=F32)


def _dotb(a, b):
    return _dot(a.astype(BF16), b.astype(BF16))


def _dot3(a, b):
    ah, al = _split2(a)
    bh, bl = _split2(b)
    return _dot(ah, bh) + (_dot(ah, bl) + _dot(al, bh))


def _dot_exact_rhs(a, b_bf16):
    a1 = a.astype(BF16)
    r1 = a - a1.astype(F32)
    a2 = r1.astype(BF16)
    a3 = (r1 - a2.astype(F32)).astype(BF16)
    return _dot(a1, b_bf16) + (_dot(a2, b_bf16) + _dot(a3, b_bf16))


def _ln_plain(x):
    mu = jnp.mean(x, axis=-1, keepdims=True)
    xc = x - mu
    var = jnp.mean(xc * xc, axis=-1, keepdims=True)
    return xc * lax.rsqrt(var + LN_EPS)


def _sigmoid(x):
    return 1.0 / (1.0 + jnp.exp(-x))


def _swap16(x):
    n = x.shape[-1]
    lane = lax.broadcasted_iota(jnp.int32, x.shape, x.ndim - 1)
    up = pltpu.roll(x, n - 16, axis=x.ndim - 1)
    dn = pltpu.roll(x, 16, axis=x.ndim - 1)
    return jnp.where((lane % 32) < 16, up, dn)


def _shift_rows(x, prev_row, next_row):
    n = x.shape[0]
    row = lax.broadcasted_iota(jnp.int32, x.shape, 0)
    prev = jnp.where(row == 0, prev_row, pltpu.roll(x, 1, axis=0))
    nxt = jnp.where(row == n - 1, next_row, pltpu.roll(x, n - 1, axis=0))
    return prev, nxt


class _Stream:
    def __init__(self, B, T, Lc):
        assert T % TM == 0 and Lc % TM == 0 and (B * T) % Lc == 0
        self.B, self.T, self.Lc = B, T, Lc
        self.n_lat_rows = B * T
        self.n_rows = B * T + B * Lc
        self.lat_tiles = B * T // TM
        self.tiles = self.n_rows // TM
        self.tpb = T // TM
        self.tpc = Lc // TM

    def mod_index(self, i):
        return jnp.where(i < self.lat_tiles, i // self.tpb, self.B)

    def seq_pos(self, i):
        j = jnp.where(i < self.lat_tiles, i % self.tpb, (i - self.lat_tiles) % self.tpc)
        n = jnp.where(i < self.lat_tiles, self.tpb, self.tpc)
        return j == 0, j == n - 1


def _halo_specs(width, col_block, n_rows):
    per = TM // HALO
    last = n_rows // HALO - 1
    prev = pl.BlockSpec((HALO, width), lambda i: (jnp.maximum(i * per - 1, 0), col_block))
    nxt = pl.BlockSpec((HALO, width), lambda i: (jnp.minimum((i + 1) * per, last), col_block))
    return prev, nxt


def _adaln_kernel(c_ref, w_ref, b_ref, o_ref):
    c = c_ref[...]
    s = c * _sigmoid(c)
    o_ref[...] = _dot3(s, w_ref[...]) + b_ref[...]


def _adaln(c_all, w, b):
    n = w.shape[1]
    tn = 1536
    return pl.pallas_call(
        _adaln_kernel,
        grid=(n // tn,),
        in_specs=[pl.BlockSpec((8, D_MODEL), lambda j: (0, 0)),
                  pl.BlockSpec((D_MODEL, tn), lambda j: (0, j)),
                  pl.BlockSpec((1, tn), lambda j: (0, j))],
        out_specs=pl.BlockSpec((8, tn), lambda j: (0, j)),
        out_shape=jax.ShapeDtypeStruct((8, n), F32),
        compiler_params=_cparams(("arbitrary",)),
        name="adaln",
    )(c_all, w, b.reshape(1, n))


def _lnmod_kernel(x_ref, mod_ref, h_ref):
    xn = _ln_plain(x_ref[...])
    shift = mod_ref[0, 0:1, :]
    scale = mod_ref[0, 1:2, :]
    h_ref[...] = (xn * (1.0 + scale) + shift).astype(BF16)


def _lnmod(x_all, mod, st, n_tiles):
    return pl.pallas_call(
        _lnmod_kernel,
        grid=(n_tiles,),
        in_specs=[pl.BlockSpec((TM, D_MODEL), lambda i: (i, 0)),
                  pl.BlockSpec((1, 6, D_MODEL), lambda i: (st.mod_index(i), 0, 0))],
        out_specs=pl.BlockSpec((TM, D_MODEL), lambda i: (i, 0)),
        out_shape=jax.ShapeDtypeStruct((n_tiles * TM, D_MODEL), BF16),
        compiler_params=_cparams(("arbitrary",)),
        name="lnmod",
    )(x_all, mod)


def _proj_kernel(a_ref, w_ref, *rest, epi):
    o_ref = rest[-1]
    acc = _dot(a_ref[...], w_ref[...])
    if epi == "rope":
        cos_ref, sin_ref = rest[0], rest[1]
        reps = acc.shape[1] // LANES
        cos = jnp.concatenate([cos_ref[...]] * reps, axis=1)
        sin = jnp.concatenate([sin_ref[...]] * reps, axis=1)
        acc = acc * cos + _swap16(acc) * sin
    elif epi == "sigmoid":
        acc = _sigmoid(acc)
    o_ref[...] = acc.astype(o_ref.dtype)


def _proj(h, w, out_dtype, epi="plain", tables=(), tn=512):
    m, k = h.shape
    n = w.shape[1]
    tn = min(tn, n)
    assert n % tn == 0 and m % TM == 0
    extra_specs = [pl.BlockSpec((TM, LANES), lambda i, j: (i, 0)) for _ in tables]
    return pl.pallas_call(
        functools.partial(_proj_kernel, epi=epi),
        grid=(m // TM, n // tn),
        in_specs=[pl.BlockSpec((TM, k), lambda i, j: (i, 0)),
                  pl.BlockSpec((k, tn), lambda i, j: (0, j))] + extra_specs,
        out_specs=pl.BlockSpec((TM, tn), lambda i, j: (i, j)),
        out_shape=jax.ShapeDtypeStruct((m, n), out_dtype),
        compiler_params=_cparams(("arbitrary", "arbitrary")),
        name="proj_" + epi,
    )(h, w, *tables)


TQ = 256
TK = 512


def _attn_kernel(lam_ref, q_ref, cos_ref, sin_ref, g_ref, kc_ref, vc_ref, *rest, n_kblk, out_scale):
    if n_kblk:
        kl_ref, vl_ref, o_ref = rest
    else:
        (o_ref,) = rest
    tq = q_ref.shape[0]
    lane = lax.broadcasted_iota(jnp.int32, (tq, LANES), 1)
    first = lane < DH_A
    q = q_ref[...] * (DH_A ** -0.5)

    def stack(x):
        return jnp.concatenate([jnp.where(first, x, 0.0), jnp.where(first, 0.0, x)], axis=0).astype(BF16)

    def block(qs, k, v, m, l, acc):
        s = _dot_nt(qs, k)
        m_new = jnp.maximum(m, jnp.max(s, axis=-1, keepdims=True))
        corr = jnp.exp(m - m_new)
        p = jnp.exp(s - m_new)
        l = l * corr + jnp.sum(p, axis=-1, keepdims=True)
        acc = acc * corr + _dot(p.astype(BF16), v)
        return m_new, l, acc

    m = jnp.full((2 * tq, 1), -1e30, F32)
    l = jnp.zeros((2 * tq, 1), F32)
    acc = jnp.zeros((2 * tq, DV_A), F32)
    m, l, acc = block(stack(q), kc_ref[...], vc_ref[...], m, l, acc)
    if n_kblk:
        q_rot = stack(q * cos_ref[...] + _swap16(q) * sin_ref[...])

        def body(j, carry):
            off = pl.multiple_of(j * TK, TK)
            return block(q_rot, kl_ref[pl.ds(off, TK), :], vl_ref[pl.ds(off, TK), :], *carry)

        m, l, acc = lax.fori_loop(0, n_kblk, body, (m, l, acc))
    o = acc / l
    o = o[:tq] - lam_ref[0, 0] * o[tq:]
    o = o * lax.rsqrt(jnp.mean(o * o, axis=-1, keepdims=True) + HEAD_NORM_EPS)
    o_ref[...] = (o * (g_ref[...] * out_scale)).astype(o_ref.dtype)


def _attention(lam, q, k, v, cos_q, sin_q, subln, st, out_scale, latent):
    B, T, Lc = st.B, st.T, st.Lc
    ctx_blk0 = B * T // Lc
    n_q = (T if latent else Lc) // TQ
    q_row0 = 0 if latent else B * T // TQ
    n_kblk = T // TK if latent else 0
    in_specs = [
        pl.BlockSpec(memory_space=pltpu.SMEM),
        pl.BlockSpec((TQ, LANES), lambda b, h, i: (q_row0 + b * n_q + i, h)),
        pl.BlockSpec((TQ, LANES), lambda b, h, i: (i if latent else 0, 0)),
        pl.BlockSpec((TQ, LANES), lambda b, h, i: (i if latent else 0, 0)),
        pl.BlockSpec((1, LANES), lambda b, h, i: (0, 0)),
        pl.BlockSpec((Lc, LANES), lambda b, h, i: (ctx_blk0 + b, h)),
        pl.BlockSpec((Lc, LANES), lambda b, h, i: (ctx_blk0 + b, h)),
    ]
    args = [lam, q, cos_q, sin_q, subln, k, v]
    if latent:
        in_specs += [pl.BlockSpec((T, LANES), lambda b, h, i: (b, h)),
                     pl.BlockSpec((T, LANES), lambda b, h, i: (b, h))]
        args += [k, v]
    return pl.pallas_call(
        functools.partial(_attn_kernel, n_kblk=n_kblk, out_scale=out_scale),
        grid=(B, HA, n_q),
        in_specs=in_specs,
        out_specs=pl.BlockSpec((TQ, LANES), lambda b, h, i: (b * n_q + i, h)),
        out_shape=jax.ShapeDtypeStruct((B * n_q * TQ, D_A), BF16),
        compiler_params=_cparams(("arbitrary", "arbitrary", "arbitrary")),
        name="attn_lat" if latent else "attn_ctx",
    )(*args)


def _dft1_kernel(f_ref, cs_ref, o_ref):
    for g in range(F_GROUPS):
        sl = slice(g * F_GROUP_W, (g + 1) * F_GROUP_W)
        r = _dot(f_ref[:, sl], cs_ref[...])
        o_ref[0, :, sl] = r[:, :F_GROUP_W].astype(o_ref.dtype)
        o_ref[1, :, sl] = r[:, F_GROUP_W:].astype(o_ref.dtype)


def _dft1(f, cs_mat):
    m = f.shape[0]
    return pl.pallas_call(
        _dft1_kernel,
        grid=(m // TM,),
        in_specs=[pl.BlockSpec((TM, D_F), lambda i: (i, 0)),
                  pl.BlockSpec((F_GROUP_W, 2 * F_GROUP_W), lambda i: (0, 0))],
        out_specs=pl.BlockSpec((2, TM, D_F), lambda i: (0, i, 0)),
        out_shape=jax.ShapeDtypeStruct((2, m, D_F), BF16),
        compiler_params=_cparams(("arbitrary",)),
        name="dft_channels",
    )(f, cs_mat)


def _dft2_kernel(a_ref, z_ref, o_ref, acc_ref, *, nk, scale):
    k = pl.program_id(2)

    @pl.when(k == 0)
    def _():
        acc_ref[...] = jnp.zeros_like(acc_ref)

    acc_ref[...] += _dot(a_ref[...], z_ref[0])

    @pl.when(k == nk - 1)
    def _():
        o_ref[...] = (acc_ref[...] * scale).astype(o_ref.dtype)


def _dft2(a_mat, z, n_seq, t_len, row0):
    tk = min(512, t_len)
    kb = t_len // tk
    nk = 2 * kb
    blk0 = row0 // tk
    scale = 1.0 / math.sqrt(t_len * F_GROUP_W)
    return pl.pallas_call(
        functools.partial(_dft2_kernel, nk=nk, scale=scale),
        grid=(n_seq, t_len // TM, nk),
        in_specs=[pl.BlockSpec((TM, tk), lambda b, i, k: (i, k)),
                  pl.BlockSpec((1, tk, D_F), lambda b, i, k: (k // kb, blk0 + b * kb + k % kb, 0))],
        out_specs=pl.BlockSpec((TM, D_F), lambda b, i, k: (b * (t_len // TM) + i, 0)),
        out_shape=jax.ShapeDtypeStruct((n_seq * t_len, D_F), BF16),
        scratch_shapes=[pltpu.VMEM((TM, D_F), F32)],
        compiler_params=_cparams(("arbitrary", "arbitrary", "arbitrary")),
        name="dft_positions",
    )(a_mat, z)


def _dft_tables(t_len):
    k = jnp.arange(t_len, dtype=jnp.int32)
    ang = ((k[:, None] * k[None, :]) % t_len).astype(F32) * (2.0 * math.pi / t_len)
    return jnp.concatenate([jnp.cos(ang), -jnp.sin(ang)], axis=1).astype(BF16)


def _rwkv_prep_kernel(cs_ref, csp_ref, csn_ref, co_ref, cop_ref, con_ref, mucs_ref, muco_ref,
                      w0_ref, w2_ref, a0_ref, a2_ref, g2_ref, kk_ref, ka_ref, rk_ref, bd_ref,
                      lw_ref, kd_ref, bb_ref, v_ref, kn_ref, r_ref, bonus_ref, g_ref, *, st):
    i = pl.program_id(0)
    first, last = st.seq_pos(i)

    def tshift(x_ref, p_ref, n_ref, mu_ref):
        x = x_ref[...]
        prev_row = jnp.where(first, 0.0, p_ref[HALO - 1:HALO, :])
        next_row = jnp.where(last, 0.0, n_ref[0:1, :])
        prev, nxt = _shift_rows(x, prev_row, next_row)
        return x + mu_ref[0:1, :] * (prev - x) + mu_ref[1:2, :] * (nxt - x)

    cs = tshift(cs_ref, csp_ref, csn_ref, mucs_ref)
    co = tshift(co_ref, cop_ref, con_ref, muco_ref)
    k = cs[:, :D_C]
    v = cs[:, D_C:2 * D_C]
    wd = cs[:, 2 * D_C:2 * D_C + 2 * LORA_W]
    ad = cs[:, 2 * D_C + 2 * LORA_W:]
    r = co[:, :D_C]
    gd = co[:, D_C:]
    bd = bd_ref[...]

    zw = _dot3(jnp.tanh(wd), w2_ref[...])
    za = _dot3(ad, a2_ref[...])
    kx = k * kk_ref[...]
    ss = _dot_exact_rhs(kx * kx, bd)
    kn = kx / jnp.maximum(jnp.sqrt(ss), 1e-12)
    kd_sum = jnp.zeros_like(k)
    for d in range(2):
        sl = slice(d * D_C, (d + 1) * D_C)
        z = -(w0_ref[d:d + 1, :] + zw[:, sl])
        softplus = jnp.maximum(z, 0.0) + jnp.log(1.0 + jnp.exp(-jnp.abs(z)))
        lw_ref[d] = -jnp.exp(-softplus - 0.5)
        a = _sigmoid(a0_ref[d:d + 1, :] + za[:, sl])
        kd = k * (1.0 + (a - 1.0) * ka_ref[...])
        kd_ref[d] = kd
        bb_ref[d] = kn * a
        kd_sum = kd_sum + kd
    v_ref[...] = v
    kn_ref[...] = kn
    r_ref[...] = r
    bonus_ref[...] = _dot_exact_rhs(r * kd_sum * rk_ref[...], bd) * v
    g_ref[...] = _dot3(_sigmoid(gd), g2_ref[...])


def _rwkv_prep(cs, co, lp, st):
    n = st.n_rows
    row = lambda i: (i, 0)
    full = lambda shape: pl.BlockSpec(shape, lambda i: (0,) * len(shape))
    csp, csn = _halo_specs(D_CS, 0, n)
    cop, con = _halo_specs(D_CO, 0, n)
    dir_out = pl.BlockSpec((2, TM, D_C), lambda i: (0, i, 0))
    tok_out = pl.BlockSpec((TM, D_C), row)
    return pl.pallas_call(
        functools.partial(_rwkv_prep_kernel, st=st),
        grid=(st.tiles,),
        in_specs=[pl.BlockSpec((TM, D_CS), row), csp, csn, pl.BlockSpec((TM, D_CO), row), cop, con,
                  full((2, D_CS)), full((2, D_CO)), full((2, D_C)), full((2 * LORA_W, 2 * D_C)),
                  full((2, D_C)), full((2 * LORA_A, 2 * D_C)), full((LORA_G, D_C)),
                  full((1, D_C)), full((1, D_C)), full((1, D_C)), full((D_C, D_C))],
        out_specs=[dir_out, dir_out, dir_out, tok_out, tok_out, tok_out, tok_out, tok_out],
        out_shape=[jax.ShapeDtypeStruct((2, n, D_C), F32)] * 3 + [jax.ShapeDtypeStruct((n, D_C), F32)] * 5,
        compiler_params=_cparams(("arbitrary",)),
        name="rwkv_prep",
    )(cs, cs, cs, co, co, co, lp["mu_cs"], lp["mu_co"], lp["w0"], lp["w2bd"], lp["a0"], lp["a2bd"],
      lp["g2"], lp["k_k"], lp["k_a"], lp["r_k"], lp["head_ones"])


def _rwkv_chunk_kernel(lw_ref, kd_ref, bb_ref, v_ref, kn_ref, r_ref, qh_ref, y0_ref, g_ref, j_ref):
    d = pl.program_id(0)
    L = CHUNK
    ti = lax.broadcasted_iota(jnp.int32, (L, L), 0)
    si = lax.broadcasted_iota(jnp.int32, (L, L), 1)
    sgn = jnp.where(d == 0, 1, -1)
    incl = jnp.where((ti - si) * sgn >= 0, 1.0, 0.0).astype(BF16)
    ri = lax.broadcasted_iota(jnp.int32, (PAIR, PAIR), 0)
    ci = lax.broadcasted_iota(jnp.int32, (PAIR, PAIR), 1)
    same_head = (ri // HS_C) == (ci // HS_C)
    lag = jnp.where(same_head, (ri % L - ci % L) * sgn, -1)
    m_strict = lag > 0
    m_incl = lag >= 0
    eye = (ri == ci).astype(F32)
    lane = lax.broadcasted_iota(jnp.int32, (L, PAIR), 1)
    head0 = lane < HS_C

    def stack(x):
        return jnp.concatenate([jnp.where(head0, x, 0.0), jnp.where(head0, 0.0, x)], axis=0)

    def fold(x):
        return x[:L] + x[L:]

    for p in range(N_PAIR):
        sl = slice(p * PAIR, (p + 1) * PAIR)
        lw = lw_ref[0, :, sl]
        kd = kd_ref[0, :, sl]
        bb = bb_ref[0, :, sl]
        v = v_ref[:, sl]
        kn = kn_ref[:, sl]
        r = r_ref[:, sl]
        cum = _dot_exact_rhs_lhs(incl, lw)
        tot = jnp.sum(lw, axis=0, keepdims=True)
        e_in = jnp.exp(cum)
        e_ex = jnp.exp(cum - lw)
        e_inv = jnp.exp(-cum)
        e_end = jnp.exp(tot - cum)
        xb = stack(kn * e_ex).astype(BF16)
        xr_f = stack(r * e_in)
        xr = xr_f.astype(BF16)
        a_t = -bb * e_inv
        k_t = kd * e_inv
        za = jnp.concatenate([a_t, a_t], axis=0).astype(BF16)
        zk = jnp.concatenate([k_t, k_t], axis=0).astype(BF16)
        vs = stack(v).astype(BF16)
        n_mat = jnp.where(m_strict, _dot_nt(xb, za), 0.0)
        m_bk = jnp.where(m_strict, _dot_nt(xb, zk), 0.0)
        m_ra = jnp.where(m_incl, _dot_nt(xr, za), 0.0).astype(BF16)
        m_rk = jnp.where(m_incl, _dot_nt(xr, zk), 0.0).astype(BF16)
        t_mat = eye + n_mat
        n_pow = n_mat
        span = 2
        while span < L:
            n_pow = _dotb(n_pow, n_pow)
            t_mat = _dotb(t_mat, eye + n_pow)
            span *= 2
        t_b = t_mat.astype(BF16)
        w = _dot(t_b, xb)
        uv = _dot(t_b, _dot(m_bk.astype(BF16), vs).astype(BF16))
        w_b = w.astype(BF16)
        uv_b = uv.astype(BF16)
        qh = xr_f + _dot(m_ra, w_b)
        y0 = _dot(m_ra, uv_b) + _dot(m_rk, vs)
        qh_ref[0, 0, 0, :, sl] = fold(qh)
        y0_ref[0, 0, 0, :, sl] = fold(y0)
        w_p = fold(w).astype(BF16)
        uv_p = fold(uv).astype(BF16)
        a_h = (-bb * e_end).astype(BF16)
        k_h = (kd * e_end).astype(BF16)
        gm = jnp.where(same_head, _dot_tn(a_h, w_p), 0.0) + eye * jnp.exp(tot)
        jm = jnp.where(same_head, _dot_tn(a_h, uv_p) + _dot_tn(k_h, v.astype(BF16)), 0.0)
        g_ref[0, 0, 0, p] = gm
        j_ref[0, 0, 0, p] = jm


def _dot_exact_rhs_lhs(mask_bf16, x):
    x1 = x.astype(BF16)
    r1 = x - x1.astype(F32)
    x2 = r1.astype(BF16)
    x3 = (r1 - x2.astype(F32)).astype(BF16)
    return _dot(mask_bf16, x1) + (_dot(mask_bf16, x2) + _dot(mask_bf16, x3))


def _scan_geometry(st):
    nc_ctx = st.Lc // CHUNK
    nc_lat = st.T // CHUNK
    ns = nc_ctx + nc_lat
    ctx_base = st.n_lat_rows // CHUNK

    def row_block(d, b, s):
        in_ctx = s < nc_ctx
        j_ctx = jnp.where(d == 0, s, nc_ctx - 1 - s)
        j_lat = jnp.where(d == 0, s - nc_ctx, nc_lat - 1 - (s - nc_ctx))
        return jnp.where(in_ctx, ctx_base + b * nc_ctx + j_ctx, b * nc_lat + j_lat)

    return nc_ctx, nc_lat, ns, row_block


def _rwkv_chunks(lw, kd, bb, v, kn, r, st):
    B = st.B
    _, _, ns, row_block = _scan_geometry(st)
    dir_in = pl.BlockSpec((1, CHUNK, D_C), lambda d, b, s: (d, row_block(d, b, s), 0))
    tok_in = pl.BlockSpec((CHUNK, D_C), lambda d, b, s: (row_block(d, b, s), 0))
    row_out = pl.BlockSpec((1, 1, 1, CHUNK, D_C), lambda d, b, s: (d, s, b, 0, 0))
    mat_out = pl.BlockSpec((1, 1, 1, N_PAIR, PAIR, PAIR), lambda d, b, s: (d, s, b, 0, 0, 0))
    return pl.pallas_call(
        _rwkv_chunk_kernel,
        grid=(2, B, ns),
        in_specs=[dir_in, dir_in, dir_in, tok_in, tok_in, tok_in],
        out_specs=[row_out, row_out, mat_out, mat_out],
        out_shape=[jax.ShapeDtypeStruct((2, ns, B, CHUNK, D_C), F32)] * 2
        + [jax.ShapeDtypeStruct((2, ns, B, N_PAIR, PAIR, PAIR), F32)] * 2,
        compiler_params=_cparams(("arbitrary", "arbitrary", "arbitrary")),
        name="rwkv_chunks",
    )(lw, kd, bb, v, kn, r)


def _rwkv_carry_kernel(qh_ref, y0_ref, g_ref, j_ref, y_ref, h_ref):
    s = pl.program_id(0)

    @pl.when(s == 0)
    def _():
        h_ref[...] = jnp.zeros_like(h_ref)

    nb = qh_ref.shape[2]
    for d in range(2):
        for b in range(nb):
            for p in range(N_PAIR):
                sl = slice(p * PAIR, (p + 1) * PAIR)
                h = h_ref[d, b, p]
                lhs = jnp.concatenate([qh_ref[d, 0, b, :, sl], g_ref[d, 0, b, p]], axis=0)
                res = _dot3(lhs, h)
                y_ref[d, 0, b, :, sl] = res[:CHUNK] + y0_ref[d, 0, b, :, sl]
                h_ref[d, b, p] = res[CHUNK:] + j_ref[d, 0, b, p]


def _rwkv_carry(qh, y0, g, j, st):
    B = st.B
    ns = qh.shape[1]
    rows = pl.BlockSpec((2, 1, B, CHUNK, D_C), lambda s: (0, s, 0, 0, 0))
    mats = pl.BlockSpec((2, 1, B, N_PAIR, PAIR, PAIR), lambda s: (0, s, 0, 0, 0, 0))
    return pl.pallas_call(
        _rwkv_carry_kernel,
        grid=(ns,),
        in_specs=[rows, rows, mats, mats],
        out_specs=rows,
        out_shape=jax.ShapeDtypeStruct((2, ns, B, CHUNK, D_C), F32),
        scratch_shapes=[pltpu.VMEM((2, B, N_PAIR, PAIR, PAIR), F32)],
        compiler_params=_cparams(("arbitrary",)),
        name="rwkv_carry",
    )(qh, y0, g, j)


def _rwkv_out_kernel(yf_ref, yb_ref, bonus_ref, g_ref, lnw_ref, lnb_ref, bd_ref, o_ref):
    n = TM // CHUNK
    y = jnp.concatenate([yf_ref[0, c, 0] + yb_ref[0, n - 1 - c, 0] for c in range(n)], axis=0)
    bd = bd_ref[...]
    mu = _dot_exact_rhs(y, bd) * (1.0 / HS_C)
    yc = y - mu
    var = _dot_exact_rhs(yc * yc, bd) * (1.0 / HS_C)
    yn = yc * lax.rsqrt(var + GN_EPS) * lnw_ref[...] + lnb_ref[...]
    o_ref[...] = ((yn + bonus_ref[...]) * g_ref[...]).astype(o_ref.dtype)


def _rwkv_out(y, bonus, g, lp, st, n_tiles):
    nc_ctx, nc_lat, _, _ = _scan_geometry(st)
    n = TM // CHUNK
    assert nc_ctx % n == 0

    def seq_of(i):
        lat = i < st.lat_tiles
        b = jnp.where(lat, i // st.tpb, (i - st.lat_tiles) // st.tpc)
        j = jnp.where(lat, i % st.tpb, (i - st.lat_tiles) % st.tpc)
        return lat, b, j

    def fwd(i):
        lat, b, j = seq_of(i)
        return (0, jnp.where(lat, nc_ctx // n + j, j), b, 0, 0)

    def bwd(i):
        lat, b, j = seq_of(i)
        return (1, jnp.where(lat, nc_ctx // n + (st.tpb - 1 - j), st.tpc - 1 - j), b, 0, 0)

    row = lambda i: (i, 0)
    full = lambda shape: pl.BlockSpec(shape, lambda i: (0,) * len(shape))
    return pl.pallas_call(
        _rwkv_out_kernel,
        grid=(n_tiles,),
        in_specs=[pl.BlockSpec((1, n, 1, CHUNK, D_C), fwd), pl.BlockSpec((1, n, 1, CHUNK, D_C), bwd),
                  pl.BlockSpec((TM, D_C), row), pl.BlockSpec((TM, D_C), row),
                  full((1, D_C)), full((1, D_C)), full((D_C, D_C))],
        out_specs=pl.BlockSpec((TM, D_C), row),
        out_shape=jax.ShapeDtypeStruct((n_tiles * TM, D_C), BF16),
        compiler_params=_cparams(("arbitrary",)),
        name="rwkv_out",
    )(y, y, bonus, g, lp["ln_w"], lp["ln_b"], lp["head_ones"])


def _mix_out_kernel(oa_ref, of_ref, oc_ref, gate_ref, x_ref, mod_ref, wpa_ref, wpf_ref, wpc_ref, wo_ref,
                    lng_ref, lnb_ref, x1_ref, h2_ref, *, alpha):
    g0 = gate_ref[:, 0:D_MODEL].astype(F32)
    g1 = gate_ref[:, D_MODEL:2 * D_MODEL].astype(F32)
    g2 = gate_ref[:, 2 * D_MODEL:].astype(F32)
    y = g0 * _dot(oa_ref[...], wpa_ref[...]) + g1 * _dot(of_ref[...], wpf_ref[...]) \
        + g2 * _dot(oc_ref[...], wpc_ref[...])
    mix = _dot(y.astype(BF16), wo_ref[...])
    z = alpha * x_ref[...] + mod_ref[0, 2:3, :] * mix
    x1 = _ln_plain(z) * lng_ref[...] + lnb_ref[...]
    x1_ref[...] = x1
    h2_ref[...] = (_ln_plain(x1) * (1.0 + mod_ref[0, 4:5, :]) + mod_ref[0, 3:4, :]).astype(BF16)


def _mix_out(o_a, o_f, o_c, gate, x_all, mod, lp, st, n_tiles, alpha):
    row = lambda i: (i, 0)
    full = lambda shape: pl.BlockSpec(shape, lambda i: (0,) * len(shape))
    return pl.pallas_call(
        functools.partial(_mix_out_kernel, alpha=alpha),
        grid=(n_tiles,),
        in_specs=[pl.BlockSpec((TM, D_A), row), pl.BlockSpec((TM, D_F), row), pl.BlockSpec((TM, D_C), row),
                  pl.BlockSpec((TM, N_BRANCH * D_MODEL), row), pl.BlockSpec((TM, D_MODEL), row),
                  pl.BlockSpec((1, 6, D_MODEL), lambda i: (st.mod_index(i), 0, 0)),
                  full((D_A, D_MODEL)), full((D_F, D_MODEL)), full((D_C, D_MODEL)), full((D_MODEL, D_MODEL)),
                  full((1, D_MODEL)), full((1, D_MODEL))],
        out_specs=[pl.BlockSpec((TM, D_MODEL), row), pl.BlockSpec((TM, D_MODEL), row)],
        out_shape=[jax.ShapeDtypeStruct((n_tiles * TM, D_MODEL), F32),
                   jax.ShapeDtypeStruct((n_tiles * TM, D_MODEL), BF16)],
        compiler_params=_cparams(("arbitrary",)),
        name="mix_out",
    )(o_a, o_f, o_c, gate, x_all, mod, lp["w_pa"], lp["w_pf"], lp["w_pc"], lp["w_o"], lp["ln1_g"], lp["ln1_b"])


def _ffn_down_kernel(u_ref, g_ref, gp_ref, gn_ref, x_ref, mod_ref, cw_ref, cb_ref, wd_ref, lng_ref, lnb_ref,
                     o_ref, *, st, alpha):
    i = pl.program_id(0)
    first, last = st.seq_pos(i)
    g = g_ref[...].astype(F32)
    prev_row = jnp.where(first, 0.0, gp_ref[HALO - 1:HALO, :].astype(F32))
    next_row = jnp.where(last, 0.0, gn_ref[0:1, :].astype(F32))
    prev, nxt = _shift_rows(g, prev_row, next_row)
    gc = cw_ref[0:1, :] * prev + cw_ref[1:2, :] * g + cw_ref[2:3, :] * nxt + cb_ref[...]
    act = 0.5 * gc * (1.0 + lax.erf(gc * (2.0 ** -0.5)))
    a = (act * u_ref[...].astype(F32)).astype(BF16)
    z = alpha * x_ref[...] + mod_ref[0, 5:6, :] * _dot(a, wd_ref[...])
    o_ref[...] = _ln_plain(z) * lng_ref[...] + lnb_ref[...]


def _ffn_down(ug, x1, mod, lp, st, n_tiles, alpha):
    row = lambda i: (i, 0)
    full = lambda shape: pl.BlockSpec(shape, lambda i: (0,) * len(shape))
    gp, gn = _halo_specs(D_FF, 1, ug.shape[0])
    return pl.pallas_call(
        functools.partial(_ffn_down_kernel, st=st, alpha=alpha),
        grid=(n_tiles,),
        in_specs=[pl.BlockSpec((TM, D_FF), lambda i: (i, 0)), pl.BlockSpec((TM, D_FF), lambda i: (i, 1)), gp, gn,
                  pl.BlockSpec((TM, D_MODEL), row),
                  pl.BlockSpec((1, 6, D_MODEL), lambda i: (st.mod_index(i), 0, 0)),
                  full((3, D_FF)), full((1, D_FF)), full((D_FF, D_MODEL)), full((1, D_MODEL)), full((1, D_MODEL))],
        out_specs=pl.BlockSpec((TM, D_MODEL), row),
        out_shape=jax.ShapeDtypeStruct((n_tiles * TM, D_MODEL), F32),
        compiler_params=_cparams(("arbitrary",)),
        name="ffn_down",
    )(ug, ug, ug, ug, x1, mod, lp["ffn_conv"], lp["ffn_conv_b"], lp["ffn_down"], lp["ln2_g"], lp["ln2_b"])


def _rope_tables(st):
    T = st.T
    pos = jnp.arange(T)
    rowp = (pos // GRID_W).astype(F32)
    colp = (pos % GRID_W).astype(F32)
    n_freq = DH_A // 4
    inv = ROPE_BASE ** (-jnp.arange(n_freq, dtype=F32) / n_freq)
    ar, ac = rowp[:, None] * inv, colp[:, None] * inv
    cos64 = jnp.concatenate([jnp.cos(ar), jnp.cos(ar), jnp.cos(ac), jnp.cos(ac)], axis=1)
    sin64 = jnp.concatenate([-jnp.sin(ar), jnp.sin(ar), -jnp.sin(ac), jnp.sin(ac)], axis=1)
    cos_q = jnp.concatenate([cos64, cos64], axis=1)
    sin_q = jnp.concatenate([sin64, sin64], axis=1)
    n_ctx = st.B * st.Lc
    cos_k = jnp.concatenate([jnp.tile(cos_q, (st.B, 1)), jnp.ones((n_ctx, LANES), F32)], axis=0)
    sin_k = jnp.concatenate([jnp.tile(sin_q, (st.B, 1)), jnp.zeros((n_ctx, LANES), F32)], axis=0)
    return cos_q, sin_q, cos_k, sin_k


def _block_diag2(m):
    z = jnp.zeros_like(m[0])
    return jnp.concatenate([jnp.concatenate([m[0], z], axis=1), jnp.concatenate([z, m[1]], axis=1)], axis=0)


def kernel(x, c, ctx, c_ctx, ada_w, ada_b, w_in, lam_qk, subln_g, tshift_mu, rw_w0, rw_w2, rw_a0, rw_a2, rw_g2, rw_kk, rw_ka, rw_rk, rw_lnw, rw_lnb, w_pa, w_pf, w_pc, w_o, ln1_g, ln1_b, ffn_up, ffn_conv, ffn_conv_b, ffn_down, ln2_g, ln2_b):
    B, T, D = x.shape
    Lc = ctx.shape[1]
    depth = w_in.shape[0]
    st = _Stream(B, T, Lc)
    alpha = (2.0 * depth) ** 0.25

    cos_q, sin_q, cos_k, sin_k = _rope_tables(st)
    dft_lat = _dft_tables(T)
    dft_ctx = _dft_tables(Lc)
    kf = jnp.arange(F_GROUP_W, dtype=jnp.int32)
    ang = ((kf[:, None] * kf[None, :]) % F_GROUP_W).astype(F32) * (2.0 * math.pi / F_GROUP_W)
    dft_ch = jnp.concatenate([jnp.cos(ang), jnp.sin(ang)], axis=1).astype(BF16)
    hi = jnp.arange(D_C) // HS_C
    head_ones = (hi[:, None] == hi[None, :]).astype(BF16)
    c_all = jnp.concatenate([c, c_ctx[None], jnp.zeros((8 - B - 1, D), F32)], axis=0)

    x_all = jnp.concatenate([x.reshape(B * T, D), ctx.reshape(B * Lc, D)], axis=0)
    for i in range(depth):
        last = i == depth - 1
        lam_init = 0.8 - 0.6 * math.exp(-0.3 * i)
        lq = lam_qk[i].astype(F32)
        lam = (jnp.exp(jnp.sum(lq[0] * lq[1])) - jnp.exp(jnp.sum(lq[2] * lq[3])) + lam_init).reshape(1, 1)
        wi = w_in[i].astype(BF16)
        lp = {
            "mu_cs": tshift_mu[i][:, :D_CS], "mu_co": tshift_mu[i][:, D_CS:],
            "w0": rw_w0[i], "w2bd": _block_diag2(rw_w2[i]), "a0": rw_a0[i], "a2bd": _block_diag2(rw_a2[i]),
            "g2": rw_g2[i], "k_k": rw_kk[i].reshape(1, D_C), "k_a": rw_ka[i].reshape(1, D_C),
            "r_k": rw_rk[i].reshape(1, D_C), "ln_w": rw_lnw[i].reshape(1, D_C), "ln_b": rw_lnb[i].reshape(1, D_C),
            "head_ones": head_ones,
            "w_pa": w_pa[i].astype(BF16), "w_pf": w_pf[i].astype(BF16), "w_pc": w_pc[i].astype(BF16),
            "w_o": w_o[i].astype(BF16), "ln1_g": ln1_g[i].reshape(1, D), "ln1_b": ln1_b[i].reshape(1, D),
            "ffn_conv": ffn_conv[i], "ffn_conv_b": ffn_conv_b[i].reshape(1, D_FF),
            "ffn_down": ffn_down[i].astype(BF16), "ln2_g": ln2_g[i].reshape(1, D), "ln2_b": ln2_b[i].reshape(1, D),
        }
        n_out = st.lat_tiles if last else st.tiles

        mod = _adaln(c_all, ada_w[i], ada_b[i]).reshape(8, 6, D)
        h = _lnmod(x_all, mod, st, st.tiles)

        q = _proj(h, wi[:, OFF_Q:OFF_K], F32)
        k = _proj(h, wi[:, OFF_K:OFF_V], BF16, "rope", (cos_k, sin_k))
        v = _proj(h, wi[:, OFF_V:OFF_F], BF16)
        f = _proj(h, wi[:, OFF_F:OFF_CS], BF16)
        cs = _proj(h, wi[:, OFF_CS:OFF_CO], F32, tn=D_CS // 2)
        co = _proj(h, wi[:, OFF_CO:OFF_GATE], F32, tn=D_CO)
        gate = _proj(h, wi[:, OFF_GATE:], BF16, "sigmoid")

        subln = subln_g[i].reshape(1, DV_A)
        o_a = _attention(lam, q, k, v, cos_q, sin_q, subln, st, 1.0 - lam_init, True)
        z = _dft1(f, dft_ch)
        o_f = _dft2(dft_lat, z, B, T, 0)
        if not last:
            o_a = jnp.concatenate([o_a, _attention(lam, q, k, v, cos_q, sin_q, subln, st, 1.0 - lam_init, False)], axis=0)
            o_f = jnp.concatenate([o_f, _dft2(dft_ctx, z, B, Lc, st.n_lat_rows)], axis=0)

        lw, kd, bb, rv, kn, rr, bonus, rg = _rwkv_prep(cs, co, lp, st)
        qh, y0, gm, jm = _rwkv_chunks(lw, kd, bb, rv, kn, rr, st)
        y = _rwkv_carry(qh, y0, gm, jm, st)
        o_c = _rwkv_out(y, bonus, rg, lp, st, n_out)

        x1, h2 = _mix_out(o_a, o_f, o_c, gate, x_all, mod, lp, st, n_out, alpha)
        ug = _proj(h2, ffn_up[i].astype(BF16), BF16)
        x_all = _ffn_down(ug, x1, mod, lp, st, n_out, alpha)
    return x_all[:B * T].reshape(B, T, D)
```

```python
import functools
import math

import jax
import jax.numpy as jnp
from jax import lax
from jax.experimental import pallas as pl
from jax.experimental.pallas import tpu as pltpu

F32 = jnp.float32
BF16 = jnp.bfloat16

D_MODEL = 1024
DEPTH = 2
GRID_W = 64
HA = 8
DH_A = 64
DV_A = 2 * DH_A
D_A = HA * DV_A
ROPE_BASE = 10000.0
F_GROUPS = 4
F_GROUP_W = 128
D_F = F_GROUPS * F_GROUP_W
HC = 8
HS_C = 64
D_C = HC * HS_C
LORA_W = 64
LORA_A = 64
LORA_G = 128
D_CS = 2 * D_C + 2 * LORA_W + 2 * LORA_A
D_CO = D_C + LORA_G
N_BRANCH = 3
D_FF = 2816
LN_EPS = 1e-5
GN_EPS = HS_C * 1e-5
HEAD_NORM_EPS = 1e-5

OFF_Q = 0
OFF_K = OFF_Q + D_A
OFF_V = OFF_K + D_A
OFF_F = OFF_V + D_A
OFF_CS = OFF_F + D_F
OFF_CO = OFF_CS + D_CS
OFF_GATE = OFF_CO + D_CO
D_IN = OFF_GATE + N_BRANCH * D_MODEL

LANES = 128
HALO = 16
TM = 256
TMP = 512
CHUNK = 64
PAIR = 2 * HS_C
N_PAIR = D_C // PAIR
VMEM_LIMIT = 56 * 1024 * 1024


def _cparams(sem):
    return pltpu.CompilerParams(dimension_semantics=sem, vmem_limit_bytes=VMEM_LIMIT)


def _split2(x):
    hi = x.astype(BF16)
    lo = (x - hi.astype(F32)).astype(BF16)
    return hi, lo


def _dot(a, b, dims=(((1,), (0,)), ((), ()))):
    return lax.dot_general(a, b, dims, preferred_element_type=F32)


def _dot_nt(a, b):
    return lax.dot_general(a, b, (((1,), (1,)), ((), ())), preferred_element_type=F32)


def _dot_tn(a, b):
    return lax.dot_general(a, b, (((0,), (0,)), ((), ())), preferred_element_type=F32)


def _dotb(a, b):
    return _dot(a.astype(BF16), b.astype(BF16))


def _dot3(a, b):
    ah, al = _split2(a)
    bh, bl = _split2(b)
    return _dot(ah, bh) + (_dot(ah, bl) + _dot(al, bh))


def _dot_exact_rhs(a, b_bf16):
    a1 = a.astype(BF16)
    r1 = a - a1.astype(F32)
    a2 = r1.astype(BF16)
    a3 = (r1 - a2.astype(F32)).astype(BF16)
    return _dot(a1, b_bf16) + (_dot(a2, b_bf16) + _dot(a3, b_bf16))


def _ln_plain(x):
    mu = jnp.mean(x, axis=-1, keepdims=True)
    xc = x - mu
    var = jnp.mean(xc * xc, axis=-1, keepdims=True)
    return xc * lax.rsqrt(var + LN_EPS)


def _sigmoid(x):
    return 1.0 / (1.0 + jnp.exp(-x))


def _swap16(x):
    n = x.shape[-1]
    lane = lax.broadcasted_iota(jnp.int32, x.shape, x.ndim - 1)
    up = pltpu.roll(x, n - 16, axis=x.ndim - 1)
    dn = pltpu.roll(x, 16, axis=x.ndim - 1)
    return jnp.where((lane % 32) < 16, up, dn)


def _shift_rows(x, prev_row, next_row):
    n = x.shape[0]
    row = lax.broadcasted_iota(jnp.int32, x.shape, 0)
    prev = jnp.where(row == 0, prev_row, pltpu.roll(x, 1, axis=0))
    nxt = jnp.where(row == n - 1, next_row, pltpu.roll(x, n - 1, axis=0))
    return prev, nxt


class _Stream:
    def __init__(self, B, T, Lc):
        assert T % TM == 0 and Lc % TM == 0 and (B * T) % Lc == 0
        self.B, self.T, self.Lc = B, T, Lc
        self.n_lat_rows = B * T
        self.n_rows = B * T + B * Lc
        self.lat_tiles = B * T // TM
        self.tiles = self.n_rows // TM
        self.tpb = T // TM
        self.tpc = Lc // TM

    def mod_index(self, i):
        return jnp.where(i < self.lat_tiles, i // self.tpb, self.B)

    def seq_pos(self, i):
        j = jnp.where(i < self.lat_tiles, i % self.tpb, (i - self.lat_tiles) % self.tpc)
        n = jnp.where(i < self.lat_tiles, self.tpb, self.tpc)
        return j == 0, j == n - 1


def _halo_specs(width, col_block, n_rows):
    per = TM // HALO
    last = n_rows // HALO - 1
    prev = pl.BlockSpec((HALO, width), lambda i: (jnp.maximum(i * per - 1, 0), col_block))
    nxt = pl.BlockSpec((HALO, width), lambda i: (jnp.minimum((i + 1) * per, last), col_block))
    return prev, nxt


def _adaln_kernel(c_ref, w_ref, b_ref, o_ref):
    c = c_ref[...]
    s = c * _sigmoid(c)
    o_ref[...] = _dot3(s, w_ref[...]) + b_ref[...]


def _adaln(c_all, w, b):
    n = w.shape[1]
    tn = 1536
    return pl.pallas_call(
        _adaln_kernel,
        grid=(n // tn,),
        in_specs=[pl.BlockSpec((8, D_MODEL), lambda j: (0, 0)),
                  pl.BlockSpec((D_MODEL, tn), lambda j: (0, j)),
                  pl.BlockSpec((1, tn), lambda j: (0, j))],
        out_specs=pl.BlockSpec((8, tn), lambda j: (0, j)),
        out_shape=jax.ShapeDtypeStruct((8, n), F32),
        compiler_params=_cparams(("arbitrary",)),
        name="adaln",
    )(c_all, w, b.reshape(1, n))


def _lnmod_kernel(x_ref, mod_ref, h_ref):
    xn = _ln_plain(x_ref[...])
    shift = mod_ref[0, 0:1, :]
    scale = mod_ref[0, 1:2, :]
    h_ref[...] = (xn * (1.0 + scale) + shift).astype(BF16)


def _lnmod(x_all, mod, st, n_tiles):
    return pl.pallas_call(
        _lnmod_kernel,
        grid=(n_tiles,),
        in_specs=[pl.BlockSpec((TM, D_MODEL), lambda i: (i, 0)),
                  pl.BlockSpec((1, 6, D_MODEL), lambda i: (st.mod_index(i), 0, 0))],
        out_specs=pl.BlockSpec((TM, D_MODEL), lambda i: (i, 0)),
        out_shape=jax.ShapeDtypeStruct((n_tiles * TM, D_MODEL), BF16),
        compiler_params=_cparams(("arbitrary",)),
        name="lnmod",
    )(x_all, mod)


def _proj_kernel(a_ref, w_ref, *rest, groups, n_tables):
    tables, outs = rest[:n_tables], rest[n_tables:]
    a = a_ref[...]
    off = 0
    for (width, chunk, epi), o_ref in zip(groups, outs):
        for c0 in range(0, width, chunk):
            acc = _dot(a, w_ref[:, off + c0:off + c0 + chunk])
            if epi == "rope":
                reps = chunk // LANES
                cos = jnp.concatenate([tables[0][...]] * reps, axis=1)
                sin = jnp.concatenate([tables[1][...]] * reps, axis=1)
                acc = acc * cos + _swap16(acc) * sin
            elif epi == "sigmoid":
                acc = _sigmoid(acc)
            o_ref[:, c0:c0 + chunk] = acc.astype(o_ref.dtype)
        off += width


def _proj(h, w, groups, tables=()):
    m, k = h.shape
    n = w.shape[1]
    assert m % TMP == 0 and sum(g[0] for g in groups) == n and all(g[0] % g[1] == 0 for g in groups)
    row = lambda i: (i, 0)
    return pl.pallas_call(
        functools.partial(_proj_kernel, groups=tuple(g[:3] for g in groups), n_tables=len(tables)),
        grid=(m // TMP,),
        in_specs=[pl.BlockSpec((TMP, k), row), pl.BlockSpec((k, n), lambda i: (0, 0))]
        + [pl.BlockSpec((TMP, LANES), row) for _ in tables],
        out_specs=[pl.BlockSpec((TMP, g[0]), row) for g in groups],
        out_shape=[jax.ShapeDtypeStruct((m, g[0]), g[3]) for g in groups],
        compiler_params=_cparams(("arbitrary",)),
        name="proj_" + "_".join(g[2] for g in groups),
    )(h, w, *tables)


TQ = 256
TK = 512


def _attn_kernel(lam_ref, q_ref, cos_ref, sin_ref, g_ref, kc_ref, vc_ref, *rest, n_kblk, out_scale):
    if n_kblk:
        kl_ref, vl_ref, o_ref = rest
    else:
        (o_ref,) = rest
    tq = q_ref.shape[0]
    lane = lax.broadcasted_iota(jnp.int32, (tq, LANES), 1)
    first = lane < DH_A
    q = q_ref[...] * (DH_A ** -0.5 * math.log2(math.e))

    def stack(x):
        return jnp.concatenate([jnp.where(first, x, 0.0), jnp.where(first, 0.0, x)], axis=0).astype(BF16)

    def block(qs, k, v, m, l, acc):
        s = _dot_nt(qs, k)
        m_new = jnp.maximum(m, jnp.max(s, axis=-1, keepdims=True))
        corr = jnp.exp2(m - m_new)
        p = jnp.exp2(s - m_new)
        l = l * corr + jnp.sum(p, axis=-1, keepdims=True)
        acc = acc * corr + _dot(p.astype(BF16), v)
        return m_new, l, acc

    carry = (jnp.full((2 * tq, 1), -1e30, F32), jnp.zeros((2 * tq, 1), F32), jnp.zeros((2 * tq, DV_A), F32))
    carry = block(stack(q), kc_ref[...], vc_ref[...], *carry)
    if n_kblk:
        q_rot = stack(q * cos_ref[...] + _swap16(q) * sin_ref[...])
        for j in range(n_kblk):
            carry = block(q_rot, kl_ref[j * TK:(j + 1) * TK, :], vl_ref[j * TK:(j + 1) * TK, :], *carry)
    _, l, acc = carry
    o = acc / l
    o = o[:tq] - lam_ref[0, 0] * o[tq:]
    o = o * lax.rsqrt(jnp.mean(o * o, axis=-1, keepdims=True) + HEAD_NORM_EPS)
    o_ref[...] = (o * (g_ref[...] * out_scale)).astype(o_ref.dtype)


def _attention(lam, q, k, v, cos_q, sin_q, subln, st, out_scale, latent):
    B, T, Lc = st.B, st.T, st.Lc
    ctx_blk0 = B * T // Lc
    n_q = (T if latent else Lc) // TQ
    q_row0 = 0 if latent else B * T // TQ
    n_kblk = T // TK if latent else 0
    in_specs = [
        pl.BlockSpec(memory_space=pltpu.SMEM),
        pl.BlockSpec((TQ, LANES), lambda b, h, i: (q_row0 + b * n_q + i, h)),
        pl.BlockSpec((TQ, LANES), lambda b, h, i: (i if latent else 0, 0)),
        pl.BlockSpec((TQ, LANES), lambda b, h, i: (i if latent else 0, 0)),
        pl.BlockSpec((1, LANES), lambda b, h, i: (0, 0)),
        pl.BlockSpec((Lc, LANES), lambda b, h, i: (ctx_blk0 + b, h)),
        pl.BlockSpec((Lc, LANES), lambda b, h, i: (ctx_blk0 + b, h)),
    ]
    args = [lam, q, cos_q, sin_q, subln, k, v]
    if latent:
        in_specs += [pl.BlockSpec((T, LANES), lambda b, h, i: (b, h)),
                     pl.BlockSpec((T, LANES), lambda b, h, i: (b, h))]
        args += [k, v]
    return pl.pallas_call(
        functools.partial(_attn_kernel, n_kblk=n_kblk, out_scale=out_scale),
        grid=(B, HA, n_q),
        in_specs=in_specs,
        out_specs=pl.BlockSpec((TQ, LANES), lambda b, h, i: (b * n_q + i, h)),
        out_shape=jax.ShapeDtypeStruct((B * n_q * TQ, D_A), BF16),
        compiler_params=_cparams(("arbitrary", "arbitrary", "arbitrary")),
        name="attn_lat" if latent else "attn_ctx",
    )(*args)


def _dft1_kernel(f_ref, cs_ref, o_ref):
    for g in range(F_GROUPS):
        sl = slice(g * F_GROUP_W, (g + 1) * F_GROUP_W)
        r = _dot(f_ref[:, sl], cs_ref[...])
        o_ref[0, :, sl] = r[:, :F_GROUP_W].astype(o_ref.dtype)
        o_ref[1, :, sl] = r[:, F_GROUP_W:].astype(o_ref.dtype)


def _dft1(f, cs_mat):
    m = f.shape[0]
    return pl.pallas_call(
        _dft1_kernel,
        grid=(m // TM,),
        in_specs=[pl.BlockSpec((TM, D_F), lambda i: (i, 0)),
                  pl.BlockSpec((F_GROUP_W, 2 * F_GROUP_W), lambda i: (0, 0))],
        out_specs=pl.BlockSpec((2, TM, D_F), lambda i: (0, i, 0)),
        out_shape=jax.ShapeDtypeStruct((2, m, D_F), BF16),
        compiler_params=_cparams(("arbitrary",)),
        name="dft_channels",
    )(f, cs_mat)


def _dft2_kernel(a_ref, z_ref, o_ref, acc_ref, *, nk, scale):
    k = pl.program_id(2)

    @pl.when(k == 0)
    def _():
        acc_ref[...] = jnp.zeros_like(acc_ref)

    acc_ref[...] += _dot(a_ref[...], z_ref[0])

    @pl.when(k == nk - 1)
    def _():
        o_ref[...] = (acc_ref[...] * scale).astype(o_ref.dtype)


def _dft2(a_mat, z, n_seq, t_len, row0):
    tm = min(1024, t_len)
    tk = min(2048, t_len)
    kb = t_len // tk
    nk = 2 * kb
    blk0 = row0 // tk
    scale = 1.0 / math.sqrt(t_len * F_GROUP_W)
    return pl.pallas_call(
        functools.partial(_dft2_kernel, nk=nk, scale=scale),
        grid=(n_seq, t_len // tm, nk),
        in_specs=[pl.BlockSpec((tm, tk), lambda b, i, k: (i, k)),
                  pl.BlockSpec((1, tk, D_F), lambda b, i, k: (k // kb, blk0 + b * kb + k % kb, 0))],
        out_specs=pl.BlockSpec((tm, D_F), lambda b, i, k: (b * (t_len // tm) + i, 0)),
        out_shape=jax.ShapeDtypeStruct((n_seq * t_len, D_F), BF16),
        scratch_shapes=[pltpu.VMEM((tm, D_F), F32)],
        compiler_params=_cparams(("arbitrary", "arbitrary", "arbitrary")),
        name="dft_positions",
    )(a_mat, z)


def _dft_tables(t_len):
    k = jnp.arange(t_len, dtype=jnp.int32)
    ang = ((k[:, None] * k[None, :]) % t_len).astype(F32) * (2.0 * math.pi / t_len)
    return jnp.concatenate([jnp.cos(ang), -jnp.sin(ang)], axis=1).astype(BF16)


def _rwkv_prep_kernel(cs_ref, csp_ref, csn_ref, co_ref, cop_ref, con_ref, mucs_ref, muco_ref,
                      w0_ref, w2_ref, a0_ref, a2_ref, g2_ref, kk_ref, ka_ref, rk_ref, bd_ref,
                      lw_ref, kd_ref, bb_ref, v_ref, kn_ref, r_ref, bonus_ref, g_ref, *, st):
    i = pl.program_id(0)
    first, last = st.seq_pos(i)

    def tshift(x_ref, p_ref, n_ref, mu_ref):
        x = x_ref[...]
        prev_row = jnp.where(first, 0.0, p_ref[HALO - 1:HALO, :])
        next_row = jnp.where(last, 0.0, n_ref[0:1, :])
        prev, nxt = _shift_rows(x, prev_row, next_row)
        return x + mu_ref[0:1, :] * (prev - x) + mu_ref[1:2, :] * (nxt - x)

    cs = tshift(cs_ref, csp_ref, csn_ref, mucs_ref)
    co = tshift(co_ref, cop_ref, con_ref, muco_ref)
    k = cs[:, :D_C]
    v = cs[:, D_C:2 * D_C]
    wd = cs[:, 2 * D_C:2 * D_C + 2 * LORA_W]
    ad = cs[:, 2 * D_C + 2 * LORA_W:]
    r = co[:, :D_C]
    gd = co[:, D_C:]
    bd = bd_ref[...]

    zw = _dot3(jnp.tanh(wd), w2_ref[...])
    za = _dot3(ad, a2_ref[...])
    kx = k * kk_ref[...]
    ss = _dot_exact_rhs(kx * kx, bd)
    kn = kx / jnp.maximum(jnp.sqrt(ss), 1e-12)
    kd_sum = jnp.zeros_like(k)
    for d in range(2):
        sl = slice(d * D_C, (d + 1) * D_C)
        z = -(w0_ref[d:d + 1, :] + zw[:, sl])
        softplus = jnp.maximum(z, 0.0) + jnp.log(1.0 + jnp.exp(-jnp.abs(z)))
        lw_ref[d] = -jnp.exp(-softplus - 0.5)
        a = _sigmoid(a0_ref[d:d + 1, :] + za[:, sl])
        kd = k * (1.0 + (a - 1.0) * ka_ref[...])
        kd_ref[d] = kd
        bb_ref[d] = kn * a
        kd_sum = kd_sum + kd
    v_ref[...] = v
    kn_ref[...] = kn
    r_ref[...] = r
    bonus_ref[...] = _dot_exact_rhs(r * kd_sum * rk_ref[...], bd) * v
    g_ref[...] = _dot3(_sigmoid(gd), g2_ref[...])


def _rwkv_prep(cs, co, lp, st):
    n = st.n_rows
    row = lambda i: (i, 0)
    full = lambda shape: pl.BlockSpec(shape, lambda i: (0,) * len(shape))
    csp, csn = _halo_specs(D_CS, 0, n)
    cop, con = _halo_specs(D_CO, 0, n)
    dir_out = pl.BlockSpec((2, TM, D_C), lambda i: (0, i, 0))
    tok_out = pl.BlockSpec((TM, D_C), row)
    return pl.pallas_call(
        functools.partial(_rwkv_prep_kernel, st=st),
        grid=(st.tiles,),
        in_specs=[pl.BlockSpec((TM, D_CS), row), csp, csn, pl.BlockSpec((TM, D_CO), row), cop, con,
                  full((2, D_CS)), full((2, D_CO)), full((2, D_C)), full((2 * LORA_W, 2 * D_C)),
                  full((2, D_C)), full((2 * LORA_A, 2 * D_C)), full((LORA_G, D_C)),
                  full((1, D_C)), full((1, D_C)), full((1, D_C)), full((D_C, D_C))],
        out_specs=[dir_out, dir_out, dir_out, tok_out, tok_out, tok_out, tok_out, tok_out],
        out_shape=[jax.ShapeDtypeStruct((2, n, D_C), F32)] * 3 + [jax.ShapeDtypeStruct((n, D_C), F32)] * 5,
        compiler_params=_cparams(("arbitrary",)),
        name="rwkv_prep",
    )(cs, cs, cs, co, co, co, lp["mu_cs"], lp["mu_co"], lp["w0"], lp["w2bd"], lp["a0"], lp["a2bd"],
      lp["g2"], lp["k_k"], lp["k_a"], lp["r_k"], lp["head_ones"])


def _rwkv_chunk_kernel(lw_ref, kd_ref, bb_ref, v_ref, kn_ref, r_ref, qh_ref, y0_ref, g_ref, j_ref):
    d = pl.program_id(0)
    L = CHUNK
    ti = lax.broadcasted_iota(jnp.int32, (L, L), 0)
    si = lax.broadcasted_iota(jnp.int32, (L, L), 1)
    sgn = jnp.where(d == 0, 1, -1)
    incl = jnp.where((ti - si) * sgn >= 0, 1.0, 0.0).astype(BF16)
    ri = lax.broadcasted_iota(jnp.int32, (PAIR, PAIR), 0)
    ci = lax.broadcasted_iota(jnp.int32, (PAIR, PAIR), 1)
    same_head = (ri // HS_C) == (ci // HS_C)
    lag = jnp.where(same_head, (ri % L - ci % L) * sgn, -1)
    m_strict = lag > 0
    m_incl = lag >= 0
    eye = (ri == ci).astype(F32)
    lane = lax.broadcasted_iota(jnp.int32, (L, D_C), 1)
    head0 = (lane % PAIR) < HS_C

    def stack(x):
        return jnp.concatenate([jnp.where(head0, x, 0.0), jnp.where(head0, 0.0, x)], axis=0)

    def fold(x):
        return x[:L] + x[L:]

    lw, kd, bb = lw_ref[0], kd_ref[0], bb_ref[0]
    v, kn, r = v_ref[...], kn_ref[...], r_ref[...]
    cum = _dot_exact_rhs_lhs(incl, lw)
    tot = jnp.sum(lw, axis=0, keepdims=True)
    e_inv = jnp.exp(-cum)
    e_end = jnp.exp(tot - cum)
    e_tot = jnp.exp(tot)
    xr_f = stack(r * jnp.exp(cum))
    rows = jnp.concatenate([stack(kn * jnp.exp(cum - lw)), xr_f], axis=0).astype(BF16)
    a_t = -bb * e_inv
    k_t = kd * e_inv
    cols = jnp.concatenate([a_t, a_t, k_t, k_t], axis=0).astype(BF16)
    vs = stack(v).astype(BF16)
    ends = jnp.concatenate([-bb * e_end, kd * e_end], axis=0).astype(BF16)
    v_b = v.astype(BF16)
    pairs = range(N_PAIR)
    sls = [slice(p * PAIR, (p + 1) * PAIR) for p in pairs]
    H = PAIR

    big = [_dot_nt(rows[:, sl], cols[:, sl]) for sl in sls]
    n_mat = [jnp.where(m_strict, x[:H, :H], 0.0) for x in big]
    m_bk = [jnp.where(m_strict, x[:H, H:], 0.0).astype(BF16) for x in big]
    m_ra = [jnp.where(m_incl, x[H:, :H], 0.0).astype(BF16) for x in big]
    m_rk = [jnp.where(m_incl, x[H:, H:], 0.0).astype(BF16) for x in big]
    u1 = [_dot(m_bk[p], vs[:, sls[p]]) for p in pairs]
    t_mat = [eye + x for x in n_mat]
    n_b = [x.astype(BF16) for x in n_mat]
    n_pow = [_dot(x, x).astype(BF16) for x in n_b]
    span = 2
    while span < L // 2:
        res = [_dot(jnp.concatenate([n_pow[p], t_mat[p].astype(BF16)], axis=0), n_pow[p]) for p in pairs]
        n_pow = [x[:H].astype(BF16) for x in res]
        t_mat = [t_mat[p] + res[p][H:] for p in pairs]
        span *= 2
    t_mat = [t_mat[p] + _dot(t_mat[p].astype(BF16), n_pow[p]) for p in pairs]
    wu = [_dot(t_mat[p].astype(BF16), jnp.concatenate([rows[:H, sls[p]], u1[p].astype(BF16)], axis=1)) for p in pairs]
    qy = [_dot(m_ra[p], wu[p].astype(BF16)) for p in pairs]
    y0x = [_dot(m_rk[p], vs[:, sls[p]]) for p in pairs]
    for p in pairs:
        qh_ref[0, 0, 0, :, sls[p]] = fold(xr_f[:, sls[p]] + qy[p][:, :H])
        y0_ref[0, 0, 0, :, sls[p]] = fold(qy[p][:, H:] + y0x[p])
    zeros = jnp.zeros((L, H), BF16)
    gj = [_dot_tn(ends[:, sls[p]],
                  jnp.concatenate([fold(wu[p]).astype(BF16), jnp.concatenate([zeros, v_b[:, sls[p]]], axis=1)], axis=0))
          for p in pairs]
    for p in pairs:
        g_ref[0, 0, 0, p] = jnp.where(same_head, gj[p][:, :H], 0.0) + eye * e_tot[:, sls[p]]
        j_ref[0, 0, 0, p] = jnp.where(same_head, gj[p][:, H:], 0.0)


def _dot_exact_rhs_lhs(mask_bf16, x):
    x1 = x.astype(BF16)
    r1 = x - x1.astype(F32)
    x2 = r1.astype(BF16)
    x3 = (r1 - x2.astype(F32)).astype(BF16)
    return _dot(mask_bf16, x1) + (_dot(mask_bf16, x2) + _dot(mask_bf16, x3))


def _scan_geometry(st):
    nc_ctx = st.Lc // CHUNK
    nc_lat = st.T // CHUNK
    ns = nc_ctx + nc_lat
    ctx_base = st.n_lat_rows // CHUNK

    def row_block(d, b, s):
        in_ctx = s < nc_ctx
        j_ctx = jnp.where(d == 0, s, nc_ctx - 1 - s)
        j_lat = jnp.where(d == 0, s - nc_ctx, nc_lat - 1 - (s - nc_ctx))
        return jnp.where(in_ctx, ctx_base + b * nc_ctx + j_ctx, b * nc_lat + j_lat)

    return nc_ctx, nc_lat, ns, row_block


def _rwkv_chunks(lw, kd, bb, v, kn, r, st):
    B = st.B
    _, _, ns, row_block = _scan_geometry(st)
    dir_in = pl.BlockSpec((1, CHUNK, D_C), lambda d, b, s: (d, row_block(d, b, s), 0))
    tok_in = pl.BlockSpec((CHUNK, D_C), lambda d, b, s: (row_block(d, b, s), 0))
    row_out = pl.BlockSpec((1, 1, 1, CHUNK, D_C), lambda d, b, s: (d, s, b, 0, 0))
    mat_out = pl.BlockSpec((1, 1, 1, N_PAIR, PAIR, PAIR), lambda d, b, s: (d, s, b, 0, 0, 0))
    return pl.pallas_call(
        _rwkv_chunk_kernel,
        grid=(2, B, ns),
        in_specs=[dir_in, dir_in, dir_in, tok_in, tok_in, tok_in],
        out_specs=[row_out, row_out, mat_out, mat_out],
        out_shape=[jax.ShapeDtypeStruct((2, ns, B, CHUNK, D_C), F32)] * 2
        + [jax.ShapeDtypeStruct((2, ns, B, N_PAIR, PAIR, PAIR), F32)] * 2,
        compiler_params=_cparams(("arbitrary", "arbitrary", "arbitrary")),
        name="rwkv_chunks",
    )(lw, kd, bb, v, kn, r)


def _rwkv_carry_kernel(qh_ref, y0_ref, g_ref, j_ref, y_ref, h_ref):
    s = pl.program_id(0)

    @pl.when(s == 0)
    def _():
        h_ref[...] = jnp.zeros_like(h_ref)

    nb = qh_ref.shape[2]
    for d in range(2):
        for b in range(nb):
            for p in range(N_PAIR):
                sl = slice(p * PAIR, (p + 1) * PAIR)
                h = h_ref[d, b, p]
                lhs = jnp.concatenate([qh_ref[d, 0, b, :, sl], g_ref[d, 0, b, p]], axis=0)
                res = _dot3(lhs, h)
                y_ref[d, 0, b, :, sl] = res[:CHUNK] + y0_ref[d, 0, b, :, sl]
                h_ref[d, b, p] = res[CHUNK:] + j_ref[d, 0, b, p]


def _rwkv_carry(qh, y0, g, j, st):
    B = st.B
    ns = qh.shape[1]
    rows = pl.BlockSpec((2, 1, B, CHUNK, D_C), lambda s: (0, s, 0, 0, 0))
    mats = pl.BlockSpec((2, 1, B, N_PAIR, PAIR, PAIR), lambda s: (0, s, 0, 0, 0, 0))
    return pl.pallas_call(
        _rwkv_carry_kernel,
        grid=(ns,),
        in_specs=[rows, rows, mats, mats],
        out_specs=rows,
        out_shape=jax.ShapeDtypeStruct((2, ns, B, CHUNK, D_C), F32),
        scratch_shapes=[pltpu.VMEM((2, B, N_PAIR, PAIR, PAIR), F32)],
        compiler_params=_cparams(("arbitrary",)),
        name="rwkv_carry",
    )(qh, y0, g, j)


def _rwkv_out_kernel(yf_ref, yb_ref, bonus_ref, g_ref, lnw_ref, lnb_ref, bd_ref, o_ref):
    n = TM // CHUNK
    y = jnp.concatenate([yf_ref[0, c, 0] + yb_ref[0, n - 1 - c, 0] for c in range(n)], axis=0)
    bd = bd_ref[...]
    mu = _dot_exact_rhs(y, bd) * (1.0 / HS_C)
    yc = y - mu
    var = _dot_exact_rhs(yc * yc, bd) * (1.0 / HS_C)
    yn = yc * lax.rsqrt(var + GN_EPS) * lnw_ref[...] + lnb_ref[...]
    o_ref[...] = ((yn + bonus_ref[...]) * g_ref[...]).astype(o_ref.dtype)


def _rwkv_out(y, bonus, g, lp, st, n_tiles):
    nc_ctx, nc_lat, _, _ = _scan_geometry(st)
    n = TM // CHUNK
    assert nc_ctx % n == 0

    def seq_of(i):
        lat = i < st.lat_tiles
        b = jnp.where(lat, i // st.tpb, (i - st.lat_tiles) // st.tpc)
        j = jnp.where(lat, i % st.tpb, (i - st.lat_tiles) % st.tpc)
        return lat, b, j

    def fwd(i):
        lat, b, j = seq_of(i)
        return (0, jnp.where(lat, nc_ctx // n + j, j), b, 0, 0)

    def bwd(i):
        lat, b, j = seq_of(i)
        return (1, jnp.where(lat, nc_ctx // n + (st.tpb - 1 - j), st.tpc - 1 - j), b, 0, 0)

    row = lambda i: (i, 0)
    full = lambda shape: pl.BlockSpec(shape, lambda i: (0,) * len(shape))
    return pl.pallas_call(
        _rwkv_out_kernel,
        grid=(n_tiles,),
        in_specs=[pl.BlockSpec((1, n, 1, CHUNK, D_C), fwd), pl.BlockSpec((1, n, 1, CHUNK, D_C), bwd),
                  pl.BlockSpec((TM, D_C), row), pl.BlockSpec((TM, D_C), row),
                  full((1, D_C)), full((1, D_C)), full((D_C, D_C))],
        out_specs=pl.BlockSpec((TM, D_C), row),
        out_shape=jax.ShapeDtypeStruct((n_tiles * TM, D_C), BF16),
        compiler_params=_cparams(("arbitrary",)),
        name="rwkv_out",
    )(y, y, bonus, g, lp["ln_w"], lp["ln_b"], lp["head_ones"])


def _mix_out_kernel(oa_ref, of_ref, oc_ref, gate_ref, x_ref, mod_ref, wpa_ref, wpf_ref, wpc_ref, wo_ref,
                    lng_ref, lnb_ref, x1_ref, h2_ref, *, alpha):
    g0 = gate_ref[:, 0:D_MODEL].astype(F32)
    g1 = gate_ref[:, D_MODEL:2 * D_MODEL].astype(F32)
    g2 = gate_ref[:, 2 * D_MODEL:].astype(F32)
    y = g0 * _dot(oa_ref[...], wpa_ref[...]) + g1 * _dot(of_ref[...], wpf_ref[...]) \
        + g2 * _dot(oc_ref[...], wpc_ref[...])
    mix = _dot(y.astype(BF16), wo_ref[...])
    z = alpha * x_ref[...] + mod_ref[0, 2:3, :] * mix
    x1 = _ln_plain(z) * lng_ref[...] + lnb_ref[...]
    x1_ref[...] = x1
    h2_ref[...] = (_ln_plain(x1) * (1.0 + mod_ref[0, 4:5, :]) + mod_ref[0, 3:4, :]).astype(BF16)


def _mix_out(o_a, o_f, o_c, gate, x_all, mod, lp, st, n_tiles, alpha):
    row = lambda i: (i, 0)
    full = lambda shape: pl.BlockSpec(shape, lambda i: (0,) * len(shape))
    return pl.pallas_call(
        functools.partial(_mix_out_kernel, alpha=alpha),
        grid=(n_tiles,),
        in_specs=[pl.BlockSpec((TM, D_A), row), pl.BlockSpec((TM, D_F), row), pl.BlockSpec((TM, D_C), row),
                  pl.BlockSpec((TM, N_BRANCH * D_MODEL), row), pl.BlockSpec((TM, D_MODEL), row),
                  pl.BlockSpec((1, 6, D_MODEL), lambda i: (st.mod_index(i), 0, 0)),
                  full((D_A, D_MODEL)), full((D_F, D_MODEL)), full((D_C, D_MODEL)), full((D_MODEL, D_MODEL)),
                  full((1, D_MODEL)), full((1, D_MODEL))],
        out_specs=[pl.BlockSpec((TM, D_MODEL), row), pl.BlockSpec((TM, D_MODEL), row)],
        out_shape=[jax.ShapeDtypeStruct((n_tiles * TM, D_MODEL), F32),
                   jax.ShapeDtypeStruct((n_tiles * TM, D_MODEL), BF16)],
        compiler_params=_cparams(("arbitrary",)),
        name="mix_out",
    )(o_a, o_f, o_c, gate, x_all, mod, lp["w_pa"], lp["w_pf"], lp["w_pc"], lp["w_o"], lp["ln1_g"], lp["ln1_b"])


def _ffn_down_kernel(u_ref, g_ref, gp_ref, gn_ref, x_ref, mod_ref, cw_ref, cb_ref, wd_ref, lng_ref, lnb_ref,
                     o_ref, *, st, alpha):
    i = pl.program_id(0)
    first, last = st.seq_pos(i)
    g = g_ref[...].astype(F32)
    prev_row = jnp.where(first, 0.0, gp_ref[HALO - 1:HALO, :].astype(F32))
    next_row = jnp.where(last, 0.0, gn_ref[0:1, :].astype(F32))
    prev, nxt = _shift_rows(g, prev_row, next_row)
    gc = cw_ref[0:1, :] * prev + cw_ref[1:2, :] * g + cw_ref[2:3, :] * nxt + cb_ref[...]
    act = 0.5 * gc * (1.0 + lax.erf(gc * (2.0 ** -0.5)))
    a = (act * u_ref[...].astype(F32)).astype(BF16)
    z = alpha * x_ref[...] + mod_ref[0, 5:6, :] * _dot(a, wd_ref[...])
    o_ref[...] = _ln_plain(z) * lng_ref[...] + lnb_ref[...]


def _ffn_down(ug, x1, mod, lp, st, n_tiles, alpha):
    row = lambda i: (i, 0)
    full = lambda shape: pl.BlockSpec(shape, lambda i: (0,) * len(shape))
    gp, gn = _halo_specs(D_FF, 1, ug.shape[0])
    return pl.pallas_call(
        functools.partial(_ffn_down_kernel, st=st, alpha=alpha),
        grid=(n_tiles,),
        in_specs=[pl.BlockSpec((TM, D_FF), lambda i: (i, 0)), pl.BlockSpec((TM, D_FF), lambda i: (i, 1)), gp, gn,
                  pl.BlockSpec((TM, D_MODEL), row),
                  pl.BlockSpec((1, 6, D_MODEL), lambda i: (st.mod_index(i), 0, 0)),
                  full((3, D_FF)), full((1, D_FF)), full((D_FF, D_MODEL)), full((1, D_MODEL)), full((1, D_MODEL))],
        out_specs=pl.BlockSpec((TM, D_MODEL), row),
        out_shape=jax.ShapeDtypeStruct((n_tiles * TM, D_MODEL), F32),
        compiler_params=_cparams(("arbitrary",)),
        name="ffn_down",
    )(ug, ug, ug, ug, x1, mod, lp["ffn_conv"], lp["ffn_conv_b"], lp["ffn_down"], lp["ln2_g"], lp["ln2_b"])


def _rope_tables(st):
    T = st.T
    pos = jnp.arange(T)
    rowp = (pos // GRID_W).astype(F32)
    colp = (pos % GRID_W).astype(F32)
    n_freq = DH_A // 4
    inv = ROPE_BASE ** (-jnp.arange(n_freq, dtype=F32) / n_freq)
    ar, ac = rowp[:, None] * inv, colp[:, None] * inv
    cos64 = jnp.concatenate([jnp.cos(ar), jnp.cos(ar), jnp.cos(ac), jnp.cos(ac)], axis=1)
    sin64 = jnp.concatenate([-jnp.sin(ar), jnp.sin(ar), -jnp.sin(ac), jnp.sin(ac)], axis=1)
    cos_q = jnp.concatenate([cos64, cos64], axis=1)
    sin_q = jnp.concatenate([sin64, sin64], axis=1)
    n_ctx = st.B * st.Lc
    cos_k = jnp.concatenate([jnp.tile(cos_q, (st.B, 1)), jnp.ones((n_ctx, LANES), F32)], axis=0)
    sin_k = jnp.concatenate([jnp.tile(sin_q, (st.B, 1)), jnp.zeros((n_ctx, LANES), F32)], axis=0)
    return cos_q, sin_q, cos_k, sin_k


def _block_diag2(m):
    z = jnp.zeros_like(m[0])
    return jnp.concatenate([jnp.concatenate([m[0], z], axis=1), jnp.concatenate([z, m[1]], axis=1)], axis=0)


def kernel(x, c, ctx, c_ctx, ada_w, ada_b, w_in, lam_qk, subln_g, tshift_mu, rw_w0, rw_w2, rw_a0, rw_a2, rw_g2, rw_kk, rw_ka, rw_rk, rw_lnw, rw_lnb, w_pa, w_pf, w_pc, w_o, ln1_g, ln1_b, ffn_up, ffn_conv, ffn_conv_b, ffn_down, ln2_g, ln2_b):
    B, T, D = x.shape
    Lc = ctx.shape[1]
    depth = w_in.shape[0]
    st = _Stream(B, T, Lc)
    alpha = (2.0 * depth) ** 0.25

    cos_q, sin_q, cos_k, sin_k = _rope_tables(st)
    dft_lat = _dft_tables(T)
    dft_ctx = _dft_tables(Lc)
    kf = jnp.arange(F_GROUP_W, dtype=jnp.int32)
    ang = ((kf[:, None] * kf[None, :]) % F_GROUP_W).astype(F32) * (2.0 * math.pi / F_GROUP_W)
    dft_ch = jnp.concatenate([jnp.cos(ang), jnp.sin(ang)], axis=1).astype(BF16)
    hi = jnp.arange(D_C) // HS_C
    head_ones = (hi[:, None] == hi[None, :]).astype(BF16)
    c_all = jnp.concatenate([c, c_ctx[None], jnp.zeros((8 - B - 1, D), F32)], axis=0)

    x_all = jnp.concatenate([x.reshape(B * T, D), ctx.reshape(B * Lc, D)], axis=0)
    for i in range(depth):
        last = i == depth - 1
        lam_init = 0.8 - 0.6 * math.exp(-0.3 * i)
        lq = lam_qk[i].astype(F32)
        lam = (jnp.exp(jnp.sum(lq[0] * lq[1])) - jnp.exp(jnp.sum(lq[2] * lq[3])) + lam_init).reshape(1, 1)
        wi = w_in[i].astype(BF16)
        lp = {
            "mu_cs": tshift_mu[i][:, :D_CS], "mu_co": tshift_mu[i][:, D_CS:],
            "w0": rw_w0[i], "w2bd": _block_diag2(rw_w2[i]), "a0": rw_a0[i], "a2bd": _block_diag2(rw_a2[i]),
            "g2": rw_g2[i], "k_k": rw_kk[i].reshape(1, D_C), "k_a": rw_ka[i].reshape(1, D_C),
            "r_k": rw_rk[i].reshape(1, D_C), "ln_w": rw_lnw[i].reshape(1, D_C), "ln_b": rw_lnb[i].reshape(1, D_C),
            "head_ones": head_ones,
            "w_pa": w_pa[i].astype(BF16), "w_pf": w_pf[i].astype(BF16), "w_pc": w_pc[i].astype(BF16),
            "w_o": w_o[i].astype(BF16), "ln1_g": ln1_g[i].reshape(1, D), "ln1_b": ln1_b[i].reshape(1, D),
            "ffn_conv": ffn_conv[i], "ffn_conv_b": ffn_conv_b[i].reshape(1, D_FF),
            "ffn_down": ffn_down[i].astype(BF16), "ln2_g": ln2_g[i].reshape(1, D), "ln2_b": ln2_b[i].reshape(1, D),
        }
        n_out = st.lat_tiles if last else st.tiles

        mod = _adaln(c_all, ada_w[i], ada_b[i]).reshape(8, 6, D)
        h = _lnmod(x_all, mod, st, st.tiles)

        q, k, v, f = _proj(h, wi[:, :OFF_CS], ((D_A, 512, "plain", F32), (D_A, 512, "rope", BF16),
                                                 (D_A, 512, "plain", BF16), (D_F, 512, "plain", BF16)), (cos_k, sin_k))
        cs, co, gate = _proj(h, wi[:, OFF_CS:], ((D_CS, D_CS // 2, "plain", F32), (D_CO, D_CO, "plain", F32),
                                                   (N_BRANCH * D_MODEL, 512, "sigmoid", BF16)))

        subln = subln_g[i].reshape(1, DV_A)
        o_a = _attention(lam, q, k, v, cos_q, sin_q, subln, st, 1.0 - lam_init, True)
        z = _dft1(f, dft_ch)
        o_f = _dft2(dft_lat, z, B, T, 0)
        if not last:
            o_a = jnp.concatenate([o_a, _attention(lam, q, k, v, cos_q, sin_q, subln, st, 1.0 - lam_init, False)], axis=0)
            o_f = jnp.concatenate([o_f, _dft2(dft_ctx, z, B, Lc, st.n_lat_rows)], axis=0)

        lw, kd, bb, rv, kn, rr, bonus, rg = _rwkv_prep(cs, co, lp, st)
        qh, y0, gm, jm = _rwkv_chunks(lw, kd, bb, rv, kn, rr, st)
        y = _rwkv_carry(qh, y0, gm, jm, st)
        o_c = _rwkv_out(y, bonus, rg, lp, st, n_out)

        x1, h2 = _mix_out(o_a, o_f, o_c, gate, x_all, mod, lp, st, n_out, alpha)
        (ug,) = _proj(h2, ffn_up[i].astype(BF16), ((2 * D_FF, 512, "plain", BF16),))
        x_all = _ffn_down(ug, x1, mod, lp, st, n_out, alpha)
    return x_all[:B * T].reshape(B, T, D)
```

```python
import functools
import math

import jax
import jax.numpy as jnp
from jax import lax
from jax.experimental import pallas as pl
from jax.experimental.pallas import tpu as pltpu

F32 = jnp.float32
BF16 = jnp.bfloat16

D_MODEL = 1024
DEPTH = 2
GRID_W = 64
HA = 8
DH_A = 64
DV_A = 2 * DH_A
D_A = HA * DV_A
ROPE_BASE = 10000.0
F_GROUPS = 4
F_GROUP_W = 128
D_F = F_GROUPS * F_GROUP_W
HC = 8
HS_C = 64
D_C = HC * HS_C
LORA_W = 64
LORA_A = 64
LORA_G = 128
D_CS = 2 * D_C + 2 * LORA_W + 2 * LORA_A
D_CO = D_C + LORA_G
N_BRANCH = 3
D_FF = 2816
LN_EPS = 1e-5
GN_EPS = HS_C * 1e-5
HEAD_NORM_EPS = 1e-5

OFF_Q = 0
OFF_K = OFF_Q + D_A
OFF_V = OFF_K + D_A
OFF_F = OFF_V + D_A
OFF_CS = OFF_F + D_F
OFF_CO = OFF_CS + D_CS
OFF_GATE = OFF_CO + D_CO
D_IN = OFF_GATE + N_BRANCH * D_MODEL

LANES = 128
HALO = 16
TM = 256
TMP = 512
CHUNK = 64
PAIR = 2 * HS_C
CPS = 2
N_PAIR = D_C // PAIR
VMEM_LIMIT = 56 * 1024 * 1024


def _cparams(sem):
    return pltpu.CompilerParams(dimension_semantics=sem, vmem_limit_bytes=VMEM_LIMIT)


def _split2(x):
    hi = x.astype(BF16)
    lo = (x - hi.astype(F32)).astype(BF16)
    return hi, lo


def _dot(a, b, dims=(((1,), (0,)), ((), ()))):
    return lax.dot_general(a, b, dims, preferred_element_type=F32)


def _dot_nt(a, b):
    return lax.dot_general(a, b, (((1,), (1,)), ((), ())), preferred_element_type=F32)


def _dot_tn(a, b):
    return lax.dot_general(a, b, (((0,), (0,)), ((), ())), preferred_element_type=F32)


def _dotb(a, b):
    return _dot(a.astype(BF16), b.astype(BF16))


def _dot3(a, b):
    ah, al = _split2(a)
    bh, bl = _split2(b)
    return _dot(ah, bh) + (_dot(ah, bl) + _dot(al, bh))


def _dot_exact_rhs(a, b_bf16):
    a1 = a.astype(BF16)
    r1 = a - a1.astype(F32)
    a2 = r1.astype(BF16)
    a3 = (r1 - a2.astype(F32)).astype(BF16)
    return _dot(a1, b_bf16) + (_dot(a2, b_bf16) + _dot(a3, b_bf16))


def _ln_plain(x):
    mu = jnp.mean(x, axis=-1, keepdims=True)
    xc = x - mu
    var = jnp.mean(xc * xc, axis=-1, keepdims=True)
    return xc * lax.rsqrt(var + LN_EPS)


def _sigmoid(x):
    return 1.0 / (1.0 + jnp.exp(-x))


def _swap16(x):
    n = x.shape[-1]
    lane = lax.broadcasted_iota(jnp.int32, x.shape, x.ndim - 1)
    up = pltpu.roll(x, n - 16, axis=x.ndim - 1)
    dn = pltpu.roll(x, 16, axis=x.ndim - 1)
    return jnp.where((lane % 32) < 16, up, dn)


def _shift_rows(x, prev_row, next_row):
    n = x.shape[0]
    row = lax.broadcasted_iota(jnp.int32, x.shape, 0)
    prev = jnp.where(row == 0, prev_row, pltpu.roll(x, 1, axis=0))
    nxt = jnp.where(row == n - 1, next_row, pltpu.roll(x, n - 1, axis=0))
    return prev, nxt


class _Stream:
    def __init__(self, B, T, Lc):
        assert T % TM == 0 and Lc % TM == 0 and (B * T) % Lc == 0
        self.B, self.T, self.Lc = B, T, Lc
        self.n_lat_rows = B * T
        self.n_rows = B * T + B * Lc
        self.lat_tiles = B * T // TM
        self.tiles = self.n_rows // TM
        self.tpb = T // TM
        self.tpc = Lc // TM

    def mod_index(self, i):
        return jnp.where(i < self.lat_tiles, i // self.tpb, self.B)

    def seq_pos(self, i):
        j = jnp.where(i < self.lat_tiles, i % self.tpb, (i - self.lat_tiles) % self.tpc)
        n = jnp.where(i < self.lat_tiles, self.tpb, self.tpc)
        return j == 0, j == n - 1


def _halo_specs(width, col_block, n_rows):
    per = TM // HALO
    last = n_rows // HALO - 1
    prev = pl.BlockSpec((HALO, width), lambda i: (jnp.maximum(i * per - 1, 0), col_block))
    nxt = pl.BlockSpec((HALO, width), lambda i: (jnp.minimum((i + 1) * per, last), col_block))
    return prev, nxt


def _adaln_kernel(c_ref, w_ref, b_ref, o_ref):
    c = c_ref[...]
    s = c * _sigmoid(c)
    o_ref[...] = _dot3(s, w_ref[...]) + b_ref[...]


def _adaln(c_all, w, b):
    n = w.shape[1]
    tn = 1536
    return pl.pallas_call(
        _adaln_kernel,
        grid=(n // tn,),
        in_specs=[pl.BlockSpec((8, D_MODEL), lambda j: (0, 0)),
                  pl.BlockSpec((D_MODEL, tn), lambda j: (0, j)),
                  pl.BlockSpec((1, tn), lambda j: (0, j))],
        out_specs=pl.BlockSpec((8, tn), lambda j: (0, j)),
        out_shape=jax.ShapeDtypeStruct((8, n), F32),
        compiler_params=_cparams(("arbitrary",)),
        name="adaln",
    )(c_all, w, b.reshape(1, n))


def _lnmod_kernel(x_ref, mod_ref, h_ref):
    xn = _ln_plain(x_ref[...])
    shift = mod_ref[0, 0:1, :]
    scale = mod_ref[0, 1:2, :]
    h_ref[...] = (xn * (1.0 + scale) + shift).astype(BF16)


def _lnmod(x_all, mod, st, n_tiles):
    return pl.pallas_call(
        _lnmod_kernel,
        grid=(n_tiles,),
        in_specs=[pl.BlockSpec((TM, D_MODEL), lambda i: (i, 0)),
                  pl.BlockSpec((1, 6, D_MODEL), lambda i: (st.mod_index(i), 0, 0))],
        out_specs=pl.BlockSpec((TM, D_MODEL), lambda i: (i, 0)),
        out_shape=jax.ShapeDtypeStruct((n_tiles * TM, D_MODEL), BF16),
        compiler_params=_cparams(("arbitrary",)),
        name="lnmod",
    )(x_all, mod)


def _proj_kernel(a_ref, w_ref, *rest, groups, n_tables):
    tables, outs = rest[:n_tables], rest[n_tables:]
    a = a_ref[...]
    off = 0
    for (width, chunk, epi), o_ref in zip(groups, outs):
        for c0 in range(0, width, chunk):
            acc = _dot(a, w_ref[:, off + c0:off + c0 + chunk])
            if epi == "rope":
                reps = chunk // LANES
                cos = jnp.concatenate([tables[0][...]] * reps, axis=1)
                sin = jnp.concatenate([tables[1][...]] * reps, axis=1)
                acc = acc * cos + _swap16(acc) * sin
            elif epi == "sigmoid":
                acc = _sigmoid(acc)
            o_ref[:, c0:c0 + chunk] = acc.astype(o_ref.dtype)
        off += width


def _proj(h, w, groups, tables=()):
    m, k = h.shape
    n = w.shape[1]
    assert m % TMP == 0 and sum(g[0] for g in groups) == n and all(g[0] % g[1] == 0 for g in groups)
    row = lambda i: (i, 0)
    return pl.pallas_call(
        functools.partial(_proj_kernel, groups=tuple(g[:3] for g in groups), n_tables=len(tables)),
        grid=(m // TMP,),
        in_specs=[pl.BlockSpec((TMP, k), row), pl.BlockSpec((k, n), lambda i: (0, 0))]
        + [pl.BlockSpec((TMP, LANES), row) for _ in tables],
        out_specs=[pl.BlockSpec((TMP, g[0]), row) for g in groups],
        out_shape=[jax.ShapeDtypeStruct((m, g[0]), g[3]) for g in groups],
        compiler_params=_cparams(("arbitrary",)),
        name="proj_" + "_".join(g[2] for g in groups),
    )(h, w, *tables)


TQ = 512
TK = 512


def _attn_kernel(lam_ref, q_ref, cos_ref, sin_ref, g_ref, kc_ref, vc_ref, *rest, n_kblk, out_scale):
    if n_kblk:
        kl_ref, vl_ref, o_ref = rest
    else:
        (o_ref,) = rest
    tq = q_ref.shape[0]
    lane = lax.broadcasted_iota(jnp.int32, (tq, LANES), 1)
    first = lane < DH_A
    q = q_ref[...] * (DH_A ** -0.5 * math.log2(math.e))

    def stack(x):
        return jnp.concatenate([jnp.where(first, x, 0.0), jnp.where(first, 0.0, x)], axis=0).astype(BF16)

    def block(qs, k, v, m, acc):
        s = _dot_nt(qs, k)
        m_new = jnp.maximum(m, jnp.max(s, axis=-1, keepdims=True))
        p = jnp.exp2((s - m_new).astype(BF16))
        pv = _dot(p, jnp.concatenate([v, jnp.ones_like(v)], axis=1))
        return m_new, acc * jnp.exp2(m - m_new) + pv

    carry = (jnp.full((2 * tq, 1), -1e30, F32), jnp.zeros((2 * tq, 2 * DV_A), F32))
    carry = block(stack(q), kc_ref[...], vc_ref[...], *carry)
    if n_kblk:
        q_rot = stack(q * cos_ref[...] + _swap16(q) * sin_ref[...])
        for j in range(n_kblk):
            carry = block(q_rot, kl_ref[j * TK:(j + 1) * TK, :], vl_ref[j * TK:(j + 1) * TK, :], *carry)
    _, acc = carry
    o = acc[:, :DV_A] / acc[:, DV_A:DV_A + 1]
    o = o[:tq] - lam_ref[0, 0] * o[tq:]
    o = o * lax.rsqrt(jnp.mean(o * o, axis=-1, keepdims=True) + HEAD_NORM_EPS)
    o_ref[...] = (o * (g_ref[...] * out_scale)).astype(o_ref.dtype)


def _attention(lam, q, k, v, cos_q, sin_q, subln, st, out_scale, latent):
    B, T, Lc = st.B, st.T, st.Lc
    ctx_blk0 = B * T // Lc
    tq = min(TQ, T if latent else Lc)
    n_q = (T if latent else Lc) // tq
    q_row0 = 0 if latent else B * T // tq
    n_kblk = T // TK if latent else 0
    in_specs = [
        pl.BlockSpec(memory_space=pltpu.SMEM),
        pl.BlockSpec((tq, LANES), lambda b, h, i: (q_row0 + b * n_q + i, h)),
        pl.BlockSpec((tq, LANES), lambda b, h, i: (i if latent else 0, 0)),
        pl.BlockSpec((tq, LANES), lambda b, h, i: (i if latent else 0, 0)),
        pl.BlockSpec((1, LANES), lambda b, h, i: (0, 0)),
        pl.BlockSpec((Lc, LANES), lambda b, h, i: (ctx_blk0 + b, h)),
        pl.BlockSpec((Lc, LANES), lambda b, h, i: (ctx_blk0 + b, h)),
    ]
    args = [lam, q, cos_q, sin_q, subln, k, v]
    if latent:
        in_specs += [pl.BlockSpec((T, LANES), lambda b, h, i: (b, h)),
                     pl.BlockSpec((T, LANES), lambda b, h, i: (b, h))]
        args += [k, v]
    return pl.pallas_call(
        functools.partial(_attn_kernel, n_kblk=n_kblk, out_scale=out_scale),
        grid=(B, HA, n_q),
        in_specs=in_specs,
        out_specs=pl.BlockSpec((tq, LANES), lambda b, h, i: (b * n_q + i, h)),
        out_shape=jax.ShapeDtypeStruct((B * n_q * tq, D_A), BF16),
        compiler_params=_cparams(("arbitrary", "arbitrary", "arbitrary")),
        name="attn_lat" if latent else "attn_ctx",
    )(*args)


def _dft1_kernel(f_ref, cs_ref, o_ref):
    for g in range(F_GROUPS):
        sl = slice(g * F_GROUP_W, (g + 1) * F_GROUP_W)
        r = _dot(f_ref[:, sl], cs_ref[...])
        o_ref[0, :, sl] = r[:, :F_GROUP_W].astype(o_ref.dtype)
        o_ref[1, :, sl] = r[:, F_GROUP_W:].astype(o_ref.dtype)


def _dft1(f, cs_mat):
    m = f.shape[0]
    return pl.pallas_call(
        _dft1_kernel,
        grid=(m // TM,),
        in_specs=[pl.BlockSpec((TM, D_F), lambda i: (i, 0)),
                  pl.BlockSpec((F_GROUP_W, 2 * F_GROUP_W), lambda i: (0, 0))],
        out_specs=pl.BlockSpec((2, TM, D_F), lambda i: (0, i, 0)),
        out_shape=jax.ShapeDtypeStruct((2, m, D_F), BF16),
        compiler_params=_cparams(("arbitrary",)),
        name="dft_channels",
    )(f, cs_mat)


def _dft2_kernel(a_ref, z_ref, o_ref, acc_ref, *, nk, scale):
    k = pl.program_id(2)

    @pl.when(k == 0)
    def _():
        acc_ref[...] = jnp.zeros_like(acc_ref)

    acc_ref[...] += _dot(a_ref[...], z_ref[0])

    @pl.when(k == nk - 1)
    def _():
        o_ref[...] = (acc_ref[...] * scale).astype(o_ref.dtype)


def _dft2(a_mat, z, n_seq, t_len, row0):
    tm = min(1024, t_len)
    tk = min(2048, t_len)
    kb = t_len // tk
    nk = 2 * kb
    blk0 = row0 // tk
    scale = 1.0 / math.sqrt(t_len * F_GROUP_W)
    return pl.pallas_call(
        functools.partial(_dft2_kernel, nk=nk, scale=scale),
        grid=(n_seq, t_len // tm, nk),
        in_specs=[pl.BlockSpec((tm, tk), lambda b, i, k: (i, k)),
                  pl.BlockSpec((1, tk, D_F), lambda b, i, k: (k // kb, blk0 + b * kb + k % kb, 0))],
        out_specs=pl.BlockSpec((tm, D_F), lambda b, i, k: (b * (t_len // tm) + i, 0)),
        out_shape=jax.ShapeDtypeStruct((n_seq * t_len, D_F), BF16),
        scratch_shapes=[pltpu.VMEM((tm, D_F), F32)],
        compiler_params=_cparams(("arbitrary", "arbitrary", "arbitrary")),
        name="dft_positions",
    )(a_mat, z)


def _dft_tables(t_len):
    n_hi = t_len // LANES
    k = jnp.arange(t_len, dtype=jnp.int32)[:, None]
    unit = 2.0 * math.pi / t_len
    a_hi = ((k * (jnp.arange(n_hi, dtype=jnp.int32) * LANES)[None, :]) % t_len).astype(F32) * unit
    a_lo = ((k * jnp.arange(LANES, dtype=jnp.int32)[None, :]) % t_len).astype(F32) * unit

    def table_kernel(c1_ref, s1_ref, c2_ref, s2_ref, o_ref):
        c2, s2 = c2_ref[...], s2_ref[...]
        for hi in range(n_hi):
            c1, s1 = c1_ref[:, hi:hi + 1], s1_ref[:, hi:hi + 1]
            o_ref[:, hi * LANES:(hi + 1) * LANES] = (c1 * c2 - s1 * s2).astype(BF16)
            o_ref[:, t_len + hi * LANES:t_len + (hi + 1) * LANES] = (-(s1 * c2 + c1 * s2)).astype(BF16)

    row = lambda i: (i, 0)
    return pl.pallas_call(
        table_kernel,
        grid=(t_len // TM,),
        in_specs=[pl.BlockSpec((TM, n_hi), row), pl.BlockSpec((TM, n_hi), row),
                  pl.BlockSpec((TM, LANES), row), pl.BlockSpec((TM, LANES), row)],
        out_specs=pl.BlockSpec((TM, 2 * t_len), row),
        out_shape=jax.ShapeDtypeStruct((t_len, 2 * t_len), BF16),
        compiler_params=_cparams(("arbitrary",)),
        name="dft_table",
    )(jnp.cos(a_hi), jnp.sin(a_hi), jnp.cos(a_lo), jnp.sin(a_lo))


def _rwkv_prep_kernel(cs_ref, csp_ref, csn_ref, co_ref, cop_ref, con_ref, mucs_ref, muco_ref,
                      w0_ref, w2_ref, a0_ref, a2_ref, g2_ref, kk_ref, ka_ref, rk_ref, bd_ref,
                      lw_ref, kd_ref, bb_ref, v_ref, kn_ref, r_ref, bonus_ref, g_ref, *, st):
    i = pl.program_id(0)
    first, last = st.seq_pos(i)

    def tshift(x_ref, p_ref, n_ref, mu_ref):
        x = x_ref[...]
        prev_row = jnp.where(first, 0.0, p_ref[HALO - 1:HALO, :])
        next_row = jnp.where(last, 0.0, n_ref[0:1, :])
        prev, nxt = _shift_rows(x, prev_row, next_row)
        return x + mu_ref[0:1, :] * (prev - x) + mu_ref[1:2, :] * (nxt - x)

    cs = tshift(cs_ref, csp_ref, csn_ref, mucs_ref)
    co = tshift(co_ref, cop_ref, con_ref, muco_ref)
    k = cs[:, :D_C]
    v = cs[:, D_C:2 * D_C]
    wd = cs[:, 2 * D_C:2 * D_C + 2 * LORA_W]
    ad = cs[:, 2 * D_C + 2 * LORA_W:]
    r = co[:, :D_C]
    gd = co[:, D_C:]
    bd = bd_ref[...]

    zw = _dot3(jnp.tanh(wd), w2_ref[...])
    za = _dot3(ad, a2_ref[...])
    kx = k * kk_ref[...]
    ss = _dot_exact_rhs(kx * kx, bd)
    kn = kx / jnp.maximum(jnp.sqrt(ss), 1e-12)
    kd_sum = jnp.zeros_like(k)
    for d in range(2):
        sl = slice(d * D_C, (d + 1) * D_C)
        z = -(w0_ref[d:d + 1, :] + zw[:, sl])
        softplus = jnp.maximum(z, 0.0) + jnp.log(1.0 + jnp.exp(-jnp.abs(z)))
        lw_ref[d] = -jnp.exp(-softplus - 0.5)
        a = _sigmoid(a0_ref[d:d + 1, :] + za[:, sl])
        kd = k * (1.0 + (a - 1.0) * ka_ref[...])
        kd_ref[d] = kd
        bb_ref[d] = kn * a
        kd_sum = kd_sum + kd
    v_ref[...] = v
    kn_ref[...] = kn
    r_ref[...] = r
    bonus_ref[...] = _dot_exact_rhs(r * kd_sum * rk_ref[...], bd) * v
    g_ref[...] = _dot3(_sigmoid(gd), g2_ref[...])


def _rwkv_prep(cs, co, lp, st):
    n = st.n_rows
    row = lambda i: (i, 0)
    full = lambda shape: pl.BlockSpec(shape, lambda i: (0,) * len(shape))
    csp, csn = _halo_specs(D_CS, 0, n)
    cop, con = _halo_specs(D_CO, 0, n)
    dir_out = pl.BlockSpec((2, TM, D_C), lambda i: (0, i, 0))
    tok_out = pl.BlockSpec((TM, D_C), row)
    return pl.pallas_call(
        functools.partial(_rwkv_prep_kernel, st=st),
        grid=(st.tiles,),
        in_specs=[pl.BlockSpec((TM, D_CS), row), csp, csn, pl.BlockSpec((TM, D_CO), row), cop, con,
                  full((2, D_CS)), full((2, D_CO)), full((2, D_C)), full((2 * LORA_W, 2 * D_C)),
                  full((2, D_C)), full((2 * LORA_A, 2 * D_C)), full((LORA_G, D_C)),
                  full((1, D_C)), full((1, D_C)), full((1, D_C)), full((D_C, D_C))],
        out_specs=[dir_out, dir_out, dir_out, tok_out, tok_out, tok_out, tok_out, tok_out],
        out_shape=[jax.ShapeDtypeStruct((2, n, D_C), F32)] * 3 + [jax.ShapeDtypeStruct((n, D_C), F32)] * 5,
        compiler_params=_cparams(("arbitrary",)),
        name="rwkv_prep",
    )(cs, cs, cs, co, co, co, lp["mu_cs"], lp["mu_co"], lp["w0"], lp["w2bd"], lp["a0"], lp["a2bd"],
      lp["g2"], lp["k_k"], lp["k_a"], lp["r_k"], lp["head_ones"])


def _rwkv_chunk_kernel(lw_ref, kd_ref, bb_ref, v_ref, kn_ref, r_ref, qh_ref, y0_ref, g_ref, j_ref):
    d = pl.program_id(0)
    L = CHUNK
    ti = lax.broadcasted_iota(jnp.int32, (L, L), 0)
    si = lax.broadcasted_iota(jnp.int32, (L, L), 1)
    sgn = jnp.where(d == 0, 1, -1)
    incl = jnp.where((ti - si) * sgn >= 0, 1.0, 0.0).astype(BF16)
    ri = lax.broadcasted_iota(jnp.int32, (PAIR, PAIR), 0)
    ci = lax.broadcasted_iota(jnp.int32, (PAIR, PAIR), 1)
    same_head = (ri // HS_C) == (ci // HS_C)
    lag = jnp.where(same_head, (ri % L - ci % L) * sgn, -1)
    m_strict = lag > 0
    m_incl = lag >= 0
    eye = (ri == ci).astype(F32)
    lane = lax.broadcasted_iota(jnp.int32, (L, D_C), 1)
    head0 = (lane % PAIR) < HS_C

    def stack(x):
        return jnp.concatenate([jnp.where(head0, x, 0.0), jnp.where(head0, 0.0, x)], axis=0)

    def fold(x):
        return x[:L] + x[L:]

    pairs = range(N_PAIR)
    sls = [slice(p * PAIR, (p + 1) * PAIR) for p in pairs]
    H = PAIR
    units = [(c, p) for c in range(CPS) for p in pairs]

    def wide(c):
        rs = slice(c * L, (c + 1) * L)
        lw, kd, bb = lw_ref[0, rs, :], kd_ref[0, rs, :], bb_ref[0, rs, :]
        v, kn, r = v_ref[rs, :], kn_ref[rs, :], r_ref[rs, :]
        cum = _dot_exact_rhs_lhs(incl, lw)
        tot = jnp.sum(lw, axis=0, keepdims=True)
        e_inv = jnp.exp(-cum)
        e_end = jnp.exp(tot - cum)
        xr_f = stack(r * jnp.exp(cum))
        a_t = -bb * e_inv
        k_t = kd * e_inv
        return dict(
            e_tot=jnp.exp(tot), xr_f=xr_f,
            rows=jnp.concatenate([stack(kn * jnp.exp(cum - lw)), xr_f], axis=0).astype(BF16),
            cols=jnp.concatenate([a_t, a_t, k_t, k_t], axis=0).astype(BF16),
            vs=stack(v).astype(BF16),
            ends=jnp.concatenate([-bb * e_end, kd * e_end], axis=0).astype(BF16),
            v_b=v.astype(BF16))

    ch = [wide(c) for c in range(CPS)]
    big = [_dot_nt(ch[c]["rows"][:, sls[p]], ch[c]["cols"][:, sls[p]]) for c, p in units]
    n_mat = [jnp.where(m_strict, x[:H, :H], 0.0) for x in big]
    m_bk = [jnp.where(m_strict, x[:H, H:], 0.0).astype(BF16) for x in big]
    m_ra = [jnp.where(m_incl, x[H:, :H], 0.0).astype(BF16) for x in big]
    m_rk = [jnp.where(m_incl, x[H:, H:], 0.0).astype(BF16) for x in big]
    n_u = range(len(units))
    u1 = [_dot(m_bk[u], ch[c]["vs"][:, sls[p]]) for u, (c, p) in enumerate(units)]
    t_mat = [eye + x for x in n_mat]
    n_b = [x.astype(BF16) for x in n_mat]
    n_pow = [_dot(x, x).astype(BF16) for x in n_b]
    span = 2
    while span < L // 2:
        res = [_dot(jnp.concatenate([n_pow[u], t_mat[u].astype(BF16)], axis=0), n_pow[u]) for u in n_u]
        n_pow = [x[:H].astype(BF16) for x in res]
        t_mat = [t_mat[u] + res[u][H:] for u in n_u]
        span *= 2
    t_mat = [t_mat[u] + _dot(t_mat[u].astype(BF16), n_pow[u]) for u in n_u]
    wu = [_dot(t_mat[u].astype(BF16), jnp.concatenate([ch[c]["rows"][:H, sls[p]], u1[u].astype(BF16)], axis=1))
          for u, (c, p) in enumerate(units)]
    qy = [_dot(m_ra[u], wu[u].astype(BF16)) for u in n_u]
    y0x = [_dot(m_rk[u], ch[c]["vs"][:, sls[p]]) for u, (c, p) in enumerate(units)]
    zeros = jnp.zeros((L, H), BF16)
    gj = [_dot_tn(ch[c]["ends"][:, sls[p]],
                  jnp.concatenate([fold(wu[u]).astype(BF16),
                                   jnp.concatenate([zeros, ch[c]["v_b"][:, sls[p]]], axis=1)], axis=0))
          for u, (c, p) in enumerate(units)]
    for u, (c, p) in enumerate(units):
        pos = jnp.where(d == 0, c, CPS - 1 - c)
        qh_ref[0, pos, 0, :, sls[p]] = fold(ch[c]["xr_f"][:, sls[p]] + qy[u][:, :H])
        y0_ref[0, pos, 0, :, sls[p]] = fold(qy[u][:, H:] + y0x[u])
        g_ref[0, pos, 0, p] = jnp.where(same_head, gj[u][:, :H], 0.0) + eye * ch[c]["e_tot"][:, sls[p]]
        j_ref[0, pos, 0, p] = jnp.where(same_head, gj[u][:, H:], 0.0)


def _dot_exact_rhs_lhs(mask_bf16, x):
    x1 = x.astype(BF16)
    r1 = x - x1.astype(F32)
    x2 = r1.astype(BF16)
    x3 = (r1 - x2.astype(F32)).astype(BF16)
    return _dot(mask_bf16, x1) + (_dot(mask_bf16, x2) + _dot(mask_bf16, x3))


def _scan_geometry(st):
    nc_ctx = st.Lc // CHUNK
    nc_lat = st.T // CHUNK
    ns = nc_ctx + nc_lat
    ctx_base = st.n_lat_rows // CHUNK

    def row_block(d, b, s):
        in_ctx = s < nc_ctx
        j_ctx = jnp.where(d == 0, s, nc_ctx - 1 - s)
        j_lat = jnp.where(d == 0, s - nc_ctx, nc_lat - 1 - (s - nc_ctx))
        return jnp.where(in_ctx, ctx_base + b * nc_ctx + j_ctx, b * nc_lat + j_lat)

    return nc_ctx, nc_lat, ns, row_block


def _rwkv_chunks(lw, kd, bb, v, kn, r, st):
    B = st.B
    nc_ctx, nc_lat, ns, row_block = _scan_geometry(st)
    assert nc_ctx % CPS == 0 and nc_lat % CPS == 0
    rows = CPS * CHUNK
    blk = lambda d, b, s: row_block(d, b, s * CPS) // CPS
    dir_in = pl.BlockSpec((1, rows, D_C), lambda d, b, s: (d, blk(d, b, s), 0))
    tok_in = pl.BlockSpec((rows, D_C), lambda d, b, s: (blk(d, b, s), 0))
    row_out = pl.BlockSpec((1, CPS, 1, CHUNK, D_C), lambda d, b, s: (d, s, b, 0, 0))
    mat_out = pl.BlockSpec((1, CPS, 1, N_PAIR, PAIR, PAIR), lambda d, b, s: (d, s, b, 0, 0, 0))
    return pl.pallas_call(
        _rwkv_chunk_kernel,
        grid=(2, B, ns // CPS),
        in_specs=[dir_in, dir_in, dir_in, tok_in, tok_in, tok_in],
        out_specs=[row_out, row_out, mat_out, mat_out],
        out_shape=[jax.ShapeDtypeStruct((2, ns, B, CHUNK, D_C), F32)] * 2
        + [jax.ShapeDtypeStruct((2, ns, B, N_PAIR, PAIR, PAIR), F32)] * 2,
        compiler_params=_cparams(("arbitrary", "arbitrary", "arbitrary")),
        name="rwkv_chunks",
    )(lw, kd, bb, v, kn, r)


def _rwkv_carry_kernel(qh_ref, y0_ref, g_ref, j_ref, y_ref, h_ref):
    s = pl.program_id(0)

    @pl.when(s == 0)
    def _():
        h_ref[...] = jnp.zeros_like(h_ref)

    nb = qh_ref.shape[2]
    for d in range(2):
        for b in range(nb):
            for p in range(N_PAIR):
                sl = slice(p * PAIR, (p + 1) * PAIR)
                h = h_ref[d, b, p]
                lhs = jnp.concatenate([qh_ref[d, 0, b, :, sl], g_ref[d, 0, b, p]], axis=0)
                res = _dot3(lhs, h)
                y_ref[d, 0, b, :, sl] = res[:CHUNK] + y0_ref[d, 0, b, :, sl]
                h_ref[d, b, p] = res[CHUNK:] + j_ref[d, 0, b, p]


def _rwkv_carry(qh, y0, g, j, st):
    B = st.B
    ns = qh.shape[1]
    rows = pl.BlockSpec((2, 1, B, CHUNK, D_C), lambda s: (0, s, 0, 0, 0))
    mats = pl.BlockSpec((2, 1, B, N_PAIR, PAIR, PAIR), lambda s: (0, s, 0, 0, 0, 0))
    return pl.pallas_call(
        _rwkv_carry_kernel,
        grid=(ns,),
        in_specs=[rows, rows, mats, mats],
        out_specs=rows,
        out_shape=jax.ShapeDtypeStruct((2, ns, B, CHUNK, D_C), F32),
        scratch_shapes=[pltpu.VMEM((2, B, N_PAIR, PAIR, PAIR), F32)],
        compiler_params=_cparams(("arbitrary",)),
        name="rwkv_carry",
    )(qh, y0, g, j)


def _rwkv_out_kernel(yf_ref, yb_ref, bonus_ref, g_ref, lnw_ref, lnb_ref, bd_ref, o_ref):
    n = TM // CHUNK
    y = jnp.concatenate([yf_ref[0, c, 0] + yb_ref[0, n - 1 - c, 0] for c in range(n)], axis=0)
    bd = bd_ref[...]
    mu = _dot_exact_rhs(y, bd) * (1.0 / HS_C)
    yc = y - mu
    var = _dot_exact_rhs(yc * yc, bd) * (1.0 / HS_C)
    yn = yc * lax.rsqrt(var + GN_EPS) * lnw_ref[...] + lnb_ref[...]
    o_ref[...] = ((yn + bonus_ref[...]) * g_ref[...]).astype(o_ref.dtype)


def _rwkv_out(y, bonus, g, lp, st, n_tiles):
    nc_ctx, nc_lat, _, _ = _scan_geometry(st)
    n = TM // CHUNK
    assert nc_ctx % n == 0

    def seq_of(i):
        lat = i < st.lat_tiles
        b = jnp.where(lat, i // st.tpb, (i - st.lat_tiles) // st.tpc)
        j = jnp.where(lat, i % st.tpb, (i - st.lat_tiles) % st.tpc)
        return lat, b, j

    def fwd(i):
        lat, b, j = seq_of(i)
        return (0, jnp.where(lat, nc_ctx // n + j, j), b, 0, 0)

    def bwd(i):
        lat, b, j = seq_of(i)
        return (1, jnp.where(lat, nc_ctx // n + (st.tpb - 1 - j), st.tpc - 1 - j), b, 0, 0)

    row = lambda i: (i, 0)
    full = lambda shape: pl.BlockSpec(shape, lambda i: (0,) * len(shape))
    return pl.pallas_call(
        _rwkv_out_kernel,
        grid=(n_tiles,),
        in_specs=[pl.BlockSpec((1, n, 1, CHUNK, D_C), fwd), pl.BlockSpec((1, n, 1, CHUNK, D_C), bwd),
                  pl.BlockSpec((TM, D_C), row), pl.BlockSpec((TM, D_C), row),
                  full((1, D_C)), full((1, D_C)), full((D_C, D_C))],
        out_specs=pl.BlockSpec((TM, D_C), row),
        out_shape=jax.ShapeDtypeStruct((n_tiles * TM, D_C), BF16),
        compiler_params=_cparams(("arbitrary",)),
        name="rwkv_out",
    )(y, y, bonus, g, lp["ln_w"], lp["ln_b"], lp["head_ones"])


def _mix_out_kernel(oa_ref, of_ref, oc_ref, gate_ref, x_ref, mod_ref, wpa_ref, wpf_ref, wpc_ref, wo_ref,
                    lng_ref, lnb_ref, x1_ref, h2_ref, *, alpha):
    g0 = gate_ref[:, 0:D_MODEL].astype(F32)
    g1 = gate_ref[:, D_MODEL:2 * D_MODEL].astype(F32)
    g2 = gate_ref[:, 2 * D_MODEL:].astype(F32)
    y = g0 * _dot(oa_ref[...], wpa_ref[...]) + g1 * _dot(of_ref[...], wpf_ref[...]) \
        + g2 * _dot(oc_ref[...], wpc_ref[...])
    mix = _dot(y.astype(BF16), wo_ref[...])
    z = alpha * x_ref[...] + mod_ref[0, 2:3, :] * mix
    x1 = _ln_plain(z) * lng_ref[...] + lnb_ref[...]
    x1_ref[...] = x1
    h2_ref[...] = (_ln_plain(x1) * (1.0 + mod_ref[0, 4:5, :]) + mod_ref[0, 3:4, :]).astype(BF16)


def _mix_out(o_a, o_f, o_c, gate, x_all, mod, lp, st, n_tiles, alpha):
    row = lambda i: (i, 0)
    full = lambda shape: pl.BlockSpec(shape, lambda i: (0,) * len(shape))
    return pl.pallas_call(
        functools.partial(_mix_out_kernel, alpha=alpha),
        grid=(n_tiles,),
        in_specs=[pl.BlockSpec((TM, D_A), row), pl.BlockSpec((TM, D_F), row), pl.BlockSpec((TM, D_C), row),
                  pl.BlockSpec((TM, N_BRANCH * D_MODEL), row), pl.BlockSpec((TM, D_MODEL), row),
                  pl.BlockSpec((1, 6, D_MODEL), lambda i: (st.mod_index(i), 0, 0)),
                  full((D_A, D_MODEL)), full((D_F, D_MODEL)), full((D_C, D_MODEL)), full((D_MODEL, D_MODEL)),
                  full((1, D_MODEL)), full((1, D_MODEL))],
        out_specs=[pl.BlockSpec((TM, D_MODEL), row), pl.BlockSpec((TM, D_MODEL), row)],
        out_shape=[jax.ShapeDtypeStruct((n_tiles * TM, D_MODEL), F32),
                   jax.ShapeDtypeStruct((n_tiles * TM, D_MODEL), BF16)],
        compiler_params=_cparams(("arbitrary",)),
        name="mix_out",
    )(o_a, o_f, o_c, gate, x_all, mod, lp["w_pa"], lp["w_pf"], lp["w_pc"], lp["w_o"], lp["ln1_g"], lp["ln1_b"])


def _ffn_down_kernel(u_ref, g_ref, gp_ref, gn_ref, x_ref, mod_ref, cw_ref, cb_ref, wd_ref, lng_ref, lnb_ref,
                     o_ref, *, st, alpha):
    i = pl.program_id(0)
    first, last = st.seq_pos(i)
    g = g_ref[...].astype(F32)
    prev_row = jnp.where(first, 0.0, gp_ref[HALO - 1:HALO, :].astype(F32))
    next_row = jnp.where(last, 0.0, gn_ref[0:1, :].astype(F32))
    prev, nxt = _shift_rows(g, prev_row, next_row)
    gc = cw_ref[0:1, :] * prev + cw_ref[1:2, :] * g + cw_ref[2:3, :] * nxt + cb_ref[...]
    act = 0.5 * gc * (1.0 + lax.erf(gc * (2.0 ** -0.5)))
    a = (act * u_ref[...].astype(F32)).astype(BF16)
    z = alpha * x_ref[...] + mod_ref[0, 5:6, :] * _dot(a, wd_ref[...])
    o_ref[...] = _ln_plain(z) * lng_ref[...] + lnb_ref[...]


def _ffn_down(ug, x1, mod, lp, st, n_tiles, alpha):
    row = lambda i: (i, 0)
    full = lambda shape: pl.BlockSpec(shape, lambda i: (0,) * len(shape))
    gp, gn = _halo_specs(D_FF, 1, ug.shape[0])
    return pl.pallas_call(
        functools.partial(_ffn_down_kernel, st=st, alpha=alpha),
        grid=(n_tiles,),
        in_specs=[pl.BlockSpec((TM, D_FF), lambda i: (i, 0)), pl.BlockSpec((TM, D_FF), lambda i: (i, 1)), gp, gn,
                  pl.BlockSpec((TM, D_MODEL), row),
                  pl.BlockSpec((1, 6, D_MODEL), lambda i: (st.mod_index(i), 0, 0)),
                  full((3, D_FF)), full((1, D_FF)), full((D_FF, D_MODEL)), full((1, D_MODEL)), full((1, D_MODEL))],
        out_specs=pl.BlockSpec((TM, D_MODEL), row),
        out_shape=jax.ShapeDtypeStruct((n_tiles * TM, D_MODEL), F32),
        compiler_params=_cparams(("arbitrary",)),
        name="ffn_down",
    )(ug, ug, ug, ug, x1, mod, lp["ffn_conv"], lp["ffn_conv_b"], lp["ffn_down"], lp["ln2_g"], lp["ln2_b"])


def _rope_tables(st):
    T = st.T
    pos = jnp.arange(T)
    rowp = (pos // GRID_W).astype(F32)
    colp = (pos % GRID_W).astype(F32)
    n_freq = DH_A // 4
    inv = ROPE_BASE ** (-jnp.arange(n_freq, dtype=F32) / n_freq)
    ar, ac = rowp[:, None] * inv, colp[:, None] * inv
    cos64 = jnp.concatenate([jnp.cos(ar), jnp.cos(ar), jnp.cos(ac), jnp.cos(ac)], axis=1)
    sin64 = jnp.concatenate([-jnp.sin(ar), jnp.sin(ar), -jnp.sin(ac), jnp.sin(ac)], axis=1)
    cos_q = jnp.concatenate([cos64, cos64], axis=1)
    sin_q = jnp.concatenate([sin64, sin64], axis=1)
    n_ctx = st.B * st.Lc
    cos_k = jnp.concatenate([jnp.tile(cos_q, (st.B, 1)), jnp.ones((n_ctx, LANES), F32)], axis=0)
    sin_k = jnp.concatenate([jnp.tile(sin_q, (st.B, 1)), jnp.zeros((n_ctx, LANES), F32)], axis=0)
    return cos_q, sin_q, cos_k, sin_k


def _block_diag2(m):
    z = jnp.zeros_like(m[0])
    return jnp.concatenate([jnp.concatenate([m[0], z], axis=1), jnp.concatenate([z, m[1]], axis=1)], axis=0)


def kernel(x, c, ctx, c_ctx, ada_w, ada_b, w_in, lam_qk, subln_g, tshift_mu, rw_w0, rw_w2, rw_a0, rw_a2, rw_g2, rw_kk, rw_ka, rw_rk, rw_lnw, rw_lnb, w_pa, w_pf, w_pc, w_o, ln1_g, ln1_b, ffn_up, ffn_conv, ffn_conv_b, ffn_down, ln2_g, ln2_b):
    B, T, D = x.shape
    Lc = ctx.shape[1]
    depth = w_in.shape[0]
    st = _Stream(B, T, Lc)
    alpha = (2.0 * depth) ** 0.25

    cos_q, sin_q, cos_k, sin_k = _rope_tables(st)
    dft_lat = _dft_tables(T)
    dft_ctx = _dft_tables(Lc)
    kf = jnp.arange(F_GROUP_W, dtype=jnp.int32)
    ang = ((kf[:, None] * kf[None, :]) % F_GROUP_W).astype(F32) * (2.0 * math.pi / F_GROUP_W)
    dft_ch = jnp.concatenate([jnp.cos(ang), jnp.sin(ang)], axis=1).astype(BF16)
    hi = jnp.arange(D_C) // HS_C
    head_ones = (hi[:, None] == hi[None, :]).astype(BF16)
    c_all = jnp.concatenate([c, c_ctx[None], jnp.zeros((8 - B - 1, D), F32)], axis=0)

    x_all = jnp.concatenate([x.reshape(B * T, D), ctx.reshape(B * Lc, D)], axis=0)
    for i in range(depth):
        last = i == depth - 1
        lam_init = 0.8 - 0.6 * math.exp(-0.3 * i)
        lq = lam_qk[i].astype(F32)
        lam = (jnp.exp(jnp.sum(lq[0] * lq[1])) - jnp.exp(jnp.sum(lq[2] * lq[3])) + lam_init).reshape(1, 1)
        wi = w_in[i].astype(BF16)
        lp = {
            "mu_cs": tshift_mu[i][:, :D_CS], "mu_co": tshift_mu[i][:, D_CS:],
            "w0": rw_w0[i], "w2bd": _block_diag2(rw_w2[i]), "a0": rw_a0[i], "a2bd": _block_diag2(rw_a2[i]),
            "g2": rw_g2[i], "k_k": rw_kk[i].reshape(1, D_C), "k_a": rw_ka[i].reshape(1, D_C),
            "r_k": rw_rk[i].reshape(1, D_C), "ln_w": rw_lnw[i].reshape(1, D_C), "ln_b": rw_lnb[i].reshape(1, D_C),
            "head_ones": head_ones,
            "w_pa": w_pa[i].astype(BF16), "w_pf": w_pf[i].astype(BF16), "w_pc": w_pc[i].astype(BF16),
            "w_o": w_o[i].astype(BF16), "ln1_g": ln1_g[i].reshape(1, D), "ln1_b": ln1_b[i].reshape(1, D),
            "ffn_conv": ffn_conv[i], "ffn_conv_b": ffn_conv_b[i].reshape(1, D_FF),
            "ffn_down": ffn_down[i].astype(BF16), "ln2_g": ln2_g[i].reshape(1, D), "ln2_b": ln2_b[i].reshape(1, D),
        }
        n_out = st.lat_tiles if last else st.tiles

        mod = _adaln(c_all, ada_w[i], ada_b[i]).reshape(8, 6, D)
        h = _lnmod(x_all, mod, st, st.tiles)

        q, k, v, f = _proj(h, wi[:, :OFF_CS], ((D_A, 512, "plain", F32), (D_A, 512, "rope", BF16),
                                                 (D_A, 512, "plain", BF16), (D_F, 512, "plain", BF16)), (cos_k, sin_k))
        cs, co, gate = _proj(h, wi[:, OFF_CS:], ((D_CS, D_CS // 2, "plain", F32), (D_CO, D_CO, "plain", F32),
                                                   (N_BRANCH * D_MODEL, 512, "sigmoid", BF16)))

        subln = subln_g[i].reshape(1, DV_A)
        o_a = _attention(lam, q, k, v, cos_q, sin_q, subln, st, 1.0 - lam_init, True)
        z = _dft1(f, dft_ch)
        o_f = _dft2(dft_lat, z, B, T, 0)
        if not last:
            o_a = jnp.concatenate([o_a, _attention(lam, q, k, v, cos_q, sin_q, subln, st, 1.0 - lam_init, False)], axis=0)
            o_f = jnp.concatenate([o_f, _dft2(dft_ctx, z, B, Lc, st.n_lat_rows)], axis=0)

        lw, kd, bb, rv, kn, rr, bonus, rg = _rwkv_prep(cs, co, lp, st)
        qh, y0, gm, jm = _rwkv_chunks(lw, kd, bb, rv, kn, rr, st)
        y = _rwkv_carry(qh, y0, gm, jm, st)
        o_c = _rwkv_out(y, bonus, rg, lp, st, n_out)

        x1, h2 = _mix_out(o_a, o_f, o_c, gate, x_all, mod, lp, st, n_out, alpha)
        (ug,) = _proj(h2, ffn_up[i].astype(BF16), ((2 * D_FF, 512, "plain", BF16),))
        x_all = _ffn_down(ug, x1, mod, lp, st, n_out, alpha)
    return x_all[:B * T].reshape(B, T, D)
```

```python
import functools
import math

import jax
import jax.numpy as jnp
from jax import lax
from jax.experimental import pallas as pl
from jax.experimental.pallas import tpu as pltpu

F32 = jnp.float32
BF16 = jnp.bfloat16

D_MODEL = 1024
DEPTH = 2
GRID_W = 64
HA = 8
DH_A = 64
DV_A = 2 * DH_A
D_A = HA * DV_A
ROPE_BASE = 10000.0
F_GROUPS = 4
F_GROUP_W = 128
D_F = F_GROUPS * F_GROUP_W
HC = 8
HS_C = 64
D_C = HC * HS_C
LORA_W = 64
LORA_A = 64
LORA_G = 128
D_CS = 2 * D_C + 2 * LORA_W + 2 * LORA_A
D_CO = D_C + LORA_G
N_BRANCH = 3
D_FF = 2816
LN_EPS = 1e-5
GN_EPS = HS_C * 1e-5
HEAD_NORM_EPS = 1e-5

OFF_Q = 0
OFF_K = OFF_Q + D_A
OFF_V = OFF_K + D_A
OFF_F = OFF_V + D_A
OFF_CS = OFF_F + D_F
OFF_CO = OFF_CS + D_CS
OFF_GATE = OFF_CO + D_CO
D_IN = OFF_GATE + N_BRANCH * D_MODEL

LANES = 128
HALO = 16
TM = 256
TMP = 512
CHUNK = 64
PAIR = 2 * HS_C
CPS = 4
N_PAIR = D_C // PAIR
VMEM_LIMIT = 56 * 1024 * 1024


def _cparams(sem):
    return pltpu.CompilerParams(dimension_semantics=sem, vmem_limit_bytes=VMEM_LIMIT)


def _split2(x):
    hi = x.astype(BF16)
    lo = (x - hi.astype(F32)).astype(BF16)
    return hi, lo


def _dot(a, b, dims=(((1,), (0,)), ((), ()))):
    return lax.dot_general(a, b, dims, preferred_element_type=F32)


def _dot_nt(a, b):
    return lax.dot_general(a, b, (((1,), (1,)), ((), ())), preferred_element_type=F32)


def _dot_tn(a, b):
    return lax.dot_general(a, b, (((0,), (0,)), ((), ())), preferred_element_type=F32)


def _dotb(a, b):
    return _dot(a.astype(BF16), b.astype(BF16))


def _dot3(a, b):
    ah, al = _split2(a)
    bh, bl = _split2(b)
    return _dot(ah, bh) + (_dot(ah, bl) + _dot(al, bh))


def _dot_exact_rhs(a, b_bf16):
    a1 = a.astype(BF16)
    r1 = a - a1.astype(F32)
    a2 = r1.astype(BF16)
    a3 = (r1 - a2.astype(F32)).astype(BF16)
    return _dot(a1, b_bf16) + (_dot(a2, b_bf16) + _dot(a3, b_bf16))


def _ln_plain(x):
    mu = jnp.mean(x, axis=-1, keepdims=True)
    xc = x - mu
    var = jnp.mean(xc * xc, axis=-1, keepdims=True)
    return xc * lax.rsqrt(var + LN_EPS)


def _sigmoid(x):
    return 1.0 / (1.0 + jnp.exp(-x))


def _swap16(x):
    n = x.shape[-1]
    lane = lax.broadcasted_iota(jnp.int32, x.shape, x.ndim - 1)
    up = pltpu.roll(x, n - 16, axis=x.ndim - 1)
    dn = pltpu.roll(x, 16, axis=x.ndim - 1)
    return jnp.where((lane % 32) < 16, up, dn)


def _shift_rows(x, prev_row, next_row):
    n = x.shape[0]
    row = lax.broadcasted_iota(jnp.int32, x.shape, 0)
    prev = jnp.where(row == 0, prev_row, pltpu.roll(x, 1, axis=0))
    nxt = jnp.where(row == n - 1, next_row, pltpu.roll(x, n - 1, axis=0))
    return prev, nxt


class _Stream:
    def __init__(self, B, T, Lc):
        assert T % TM == 0 and Lc % TM == 0 and (B * T) % Lc == 0
        self.B, self.T, self.Lc = B, T, Lc
        self.n_lat_rows = B * T
        self.n_rows = B * T + B * Lc
        self.lat_tiles = B * T // TM
        self.tiles = self.n_rows // TM
        self.tpb = T // TM
        self.tpc = Lc // TM

    def mod_index(self, i):
        return jnp.where(i < self.lat_tiles, i // self.tpb, self.B)

    def seq_pos(self, i):
        j = jnp.where(i < self.lat_tiles, i % self.tpb, (i - self.lat_tiles) % self.tpc)
        n = jnp.where(i < self.lat_tiles, self.tpb, self.tpc)
        return j == 0, j == n - 1


def _halo_specs(width, col_block, n_rows):
    per = TM // HALO
    last = n_rows // HALO - 1
    prev = pl.BlockSpec((HALO, width), lambda i: (jnp.maximum(i * per - 1, 0), col_block))
    nxt = pl.BlockSpec((HALO, width), lambda i: (jnp.minimum((i + 1) * per, last), col_block))
    return prev, nxt


def _adaln_kernel(c_ref, w_ref, b_ref, o_ref):
    c = c_ref[...]
    s = c * _sigmoid(c)
    o_ref[...] = _dot3(s, w_ref[...]) + b_ref[...]


def _adaln(c_all, w, b):
    n = w.shape[1]
    tn = 1536
    return pl.pallas_call(
        _adaln_kernel,
        grid=(n // tn,),
        in_specs=[pl.BlockSpec((8, D_MODEL), lambda j: (0, 0)),
                  pl.BlockSpec((D_MODEL, tn), lambda j: (0, j)),
                  pl.BlockSpec((1, tn), lambda j: (0, j))],
        out_specs=pl.BlockSpec((8, tn), lambda j: (0, j)),
        out_shape=jax.ShapeDtypeStruct((8, n), F32),
        compiler_params=_cparams(("arbitrary",)),
        name="adaln",
    )(c_all, w, b.reshape(1, n))


def _proj_kernel(a_ref, w_ref, *rest, groups, has_mod, n_tables):
    if has_mod:
        mod_ref, rest = rest[0], rest[1:]
        a = (_ln_plain(a_ref[...]) * (1.0 + mod_ref[0, 1:2, :]) + mod_ref[0, 0:1, :]).astype(BF16)
    else:
        a = a_ref[...]
    tables, outs = rest[:n_tables], rest[n_tables:]
    off = 0
    for (width, chunk, epi), o_ref in zip(groups, outs):
        for c0 in range(0, width, chunk):
            acc = _dot(a, w_ref[:, off + c0:off + c0 + chunk])
            if epi == "rope":
                reps = chunk // LANES
                cos = jnp.concatenate([tables[0][...]] * reps, axis=1)
                sin = jnp.concatenate([tables[1][...]] * reps, axis=1)
                acc = acc * cos + _swap16(acc) * sin
            elif epi == "sigmoid":
                acc = _sigmoid(acc)
            if epi == "dft":
                fb = acc.astype(BF16)
                for g in range(chunk // F_GROUP_W):
                    sl = slice(c0 + g * F_GROUP_W, c0 + (g + 1) * F_GROUP_W)
                    r = _dot(fb[:, g * F_GROUP_W:(g + 1) * F_GROUP_W], tables[2][...])
                    o_ref[0, :, sl] = r[:, :F_GROUP_W].astype(o_ref.dtype)
                    o_ref[1, :, sl] = r[:, F_GROUP_W:].astype(o_ref.dtype)
            else:
                o_ref[:, c0:c0 + chunk] = acc.astype(o_ref.dtype)
        off += width


def _proj(h, w, groups, tables=(), mod=None, st=None):
    m, k = h.shape
    n = w.shape[1]
    assert m % TMP == 0 and sum(g[0] for g in groups) == n and all(g[0] % g[1] == 0 for g in groups)
    row = lambda i: (i, 0)
    specs = [pl.BlockSpec((TMP, k), row), pl.BlockSpec((k, n), lambda i: (0, 0))]
    args = [h, w]
    if mod is not None:
        lat_tiles, per_seq = st.n_lat_rows // TMP, st.T // TMP
        specs.append(pl.BlockSpec((1, 6, D_MODEL), lambda i: (jnp.where(i < lat_tiles, i // per_seq, st.B), 0, 0)))
        args.append(mod)
    for t in tables:
        specs.append(pl.BlockSpec((TMP, LANES), row) if t.shape[0] == m else pl.BlockSpec(t.shape, lambda i: (0, 0)))
    out_specs = [pl.BlockSpec((2, TMP, g[0]), lambda i: (0, i, 0)) if g[2] == "dft" else pl.BlockSpec((TMP, g[0]), row)
                 for g in groups]
    out_shape = [jax.ShapeDtypeStruct((2, m, g[0]) if g[2] == "dft" else (m, g[0]), g[3]) for g in groups]
    return pl.pallas_call(
        functools.partial(_proj_kernel, groups=tuple(g[:3] for g in groups), has_mod=mod is not None,
                          n_tables=len(tables)),
        grid=(m // TMP,),
        in_specs=specs,
        out_specs=out_specs,
        out_shape=out_shape,
        compiler_params=_cparams(("arbitrary",)),
        name="proj_" + "_".join(g[2] for g in groups),
    )(*args, *tables)


TQ = 512
TK = 512
AHEAD = 1


def _attn_kernel(lam_ref, q_ref, cos_ref, sin_ref, g_ref, kc_ref, vc_ref, *rest, n_kblk, out_scale):
    if n_kblk:
        kl_ref, vl_ref, o_ref = rest
    else:
        (o_ref,) = rest
    tq = q_ref.shape[0]
    lane = lax.broadcasted_iota(jnp.int32, (tq, LANES), 1)
    first = lane < DH_A
    q = q_ref[...] * (DH_A ** -0.5 * math.log2(math.e))

    def stack(x):
        return jnp.concatenate([jnp.where(first, x, 0.0), jnp.where(first, 0.0, x)], axis=0).astype(BF16)

    def update(s, v, m, acc):
        m_new = jnp.maximum(m, jnp.max(s, axis=-1, keepdims=True))
        p = jnp.exp2((s - m_new).astype(BF16))
        pv = _dot(p, jnp.concatenate([v, jnp.ones_like(v)], axis=1))
        return m_new, acc * jnp.exp2(m - m_new) + pv

    carry = (jnp.full((2 * tq, 1), -1e30, F32), jnp.zeros((2 * tq, 2 * DV_A), F32))
    pending = [(_dot_nt(stack(q), kc_ref[...]), vc_ref[...])]
    if n_kblk:
        q_rot = stack(q * cos_ref[...] + _swap16(q) * sin_ref[...])
        for j in range(n_kblk):
            pending.append((_dot_nt(q_rot, kl_ref[j * TK:(j + 1) * TK, :]), vl_ref[j * TK:(j + 1) * TK, :]))
            if len(pending) > AHEAD:
                carry = update(*pending.pop(0), *carry)
    for s, v in pending:
        carry = update(s, v, *carry)
    _, acc = carry
    o = acc[:, :DV_A] / acc[:, DV_A:DV_A + 1]
    o = o[:tq] - lam_ref[0, 0] * o[tq:]
    o = o * lax.rsqrt(jnp.mean(o * o, axis=-1, keepdims=True) + HEAD_NORM_EPS)
    o_ref[...] = (o * (g_ref[...] * out_scale)).astype(o_ref.dtype)


def _attention(lam, q, k, v, cos_q, sin_q, subln, st, out_scale, latent):
    B, T, Lc = st.B, st.T, st.Lc
    ctx_blk0 = B * T // Lc
    tq = min(TQ, T if latent else Lc)
    n_q = (T if latent else Lc) // tq
    q_row0 = 0 if latent else B * T // tq
    n_kblk = T // TK if latent else 0
    in_specs = [
        pl.BlockSpec(memory_space=pltpu.SMEM),
        pl.BlockSpec((tq, LANES), lambda b, h, i: (q_row0 + b * n_q + i, h)),
        pl.BlockSpec((tq, LANES), lambda b, h, i: (i if latent else 0, 0)),
        pl.BlockSpec((tq, LANES), lambda b, h, i: (i if latent else 0, 0)),
        pl.BlockSpec((1, LANES), lambda b, h, i: (0, 0)),
        pl.BlockSpec((Lc, LANES), lambda b, h, i: (ctx_blk0 + b, h)),
        pl.BlockSpec((Lc, LANES), lambda b, h, i: (ctx_blk0 + b, h)),
    ]
    args = [lam, q, cos_q, sin_q, subln, k, v]
    if latent:
        in_specs += [pl.BlockSpec((T, LANES), lambda b, h, i: (b, h)),
                     pl.BlockSpec((T, LANES), lambda b, h, i: (b, h))]
        args += [k, v]
    return pl.pallas_call(
        functools.partial(_attn_kernel, n_kblk=n_kblk, out_scale=out_scale),
        grid=(B, HA, n_q),
        in_specs=in_specs,
        out_specs=pl.BlockSpec((tq, LANES), lambda b, h, i: (b * n_q + i, h)),
        out_shape=jax.ShapeDtypeStruct((B * n_q * tq, D_A), BF16),
        compiler_params=_cparams(("arbitrary", "arbitrary", "arbitrary")),
        name="attn_lat" if latent else "attn_ctx",
    )(*args)


def _dft2_kernel(a_ref, z_ref, o_ref, acc_ref, *, nk, scale):
    k = pl.program_id(2)

    @pl.when(k == 0)
    def _():
        acc_ref[...] = jnp.zeros_like(acc_ref)

    acc_ref[...] += _dot(a_ref[...], z_ref[0])

    @pl.when(k == nk - 1)
    def _():
        o_ref[...] = (acc_ref[...] * scale).astype(o_ref.dtype)


def _dft2(a_mat, z, n_seq, t_len, row0):
    tm = min(1024, t_len)
    tk = min(2048, t_len)
    kb = t_len // tk
    nk = 2 * kb
    blk0 = row0 // tk
    scale = 1.0 / math.sqrt(t_len * F_GROUP_W)
    return pl.pallas_call(
        functools.partial(_dft2_kernel, nk=nk, scale=scale),
        grid=(n_seq, t_len // tm, nk),
        in_specs=[pl.BlockSpec((tm, tk), lambda b, i, k: (i, k)),
                  pl.BlockSpec((1, tk, D_F), lambda b, i, k: (k // kb, blk0 + b * kb + k % kb, 0))],
        out_specs=pl.BlockSpec((tm, D_F), lambda b, i, k: (b * (t_len // tm) + i, 0)),
        out_shape=jax.ShapeDtypeStruct((n_seq * t_len, D_F), BF16),
        scratch_shapes=[pltpu.VMEM((tm, D_F), F32)],
        compiler_params=_cparams(("arbitrary", "arbitrary", "arbitrary")),
        name="dft_positions",
    )(a_mat, z)


def _dft_tables(t_len):
    n_hi = t_len // LANES
    k = jnp.arange(t_len, dtype=jnp.int32)[:, None]
    unit = 2.0 * math.pi / t_len
    a_hi = ((k * (jnp.arange(n_hi, dtype=jnp.int32) * LANES)[None, :]) % t_len).astype(F32) * unit
    a_lo = ((k * jnp.arange(LANES, dtype=jnp.int32)[None, :]) % t_len).astype(F32) * unit

    def table_kernel(c1_ref, s1_ref, c2_ref, s2_ref, o_ref):
        c2, s2 = c2_ref[...], s2_ref[...]
        for hi in range(n_hi):
            c1, s1 = c1_ref[:, hi:hi + 1], s1_ref[:, hi:hi + 1]
            o_ref[:, hi * LANES:(hi + 1) * LANES] = (c1 * c2 - s1 * s2).astype(BF16)
            o_ref[:, t_len + hi * LANES:t_len + (hi + 1) * LANES] = (-(s1 * c2 + c1 * s2)).astype(BF16)

    row = lambda i: (i, 0)
    return pl.pallas_call(
        table_kernel,
        grid=(t_len // TM,),
        in_specs=[pl.BlockSpec((TM, n_hi), row), pl.BlockSpec((TM, n_hi), row),
                  pl.BlockSpec((TM, LANES), row), pl.BlockSpec((TM, LANES), row)],
        out_specs=pl.BlockSpec((TM, 2 * t_len), row),
        out_shape=jax.ShapeDtypeStruct((t_len, 2 * t_len), BF16),
        compiler_params=_cparams(("arbitrary",)),
        name="dft_table",
    )(jnp.cos(a_hi), jnp.sin(a_hi), jnp.cos(a_lo), jnp.sin(a_lo))


def _rwkv_prep_kernel(cs_ref, csp_ref, csn_ref, co_ref, cop_ref, con_ref, mucs_ref, muco_ref,
                      w0_ref, w2_ref, a0_ref, a2_ref, g2_ref, kk_ref, ka_ref, rk_ref, bd_ref,
                      lw_ref, kd_ref, bb_ref, v_ref, kn_ref, r_ref, bonus_ref, g_ref, *, st):
    i = pl.program_id(0)
    first, last = st.seq_pos(i)

    def tshift(x_ref, p_ref, n_ref, mu_ref):
        x = x_ref[...]
        prev_row = jnp.where(first, 0.0, p_ref[HALO - 1:HALO, :])
        next_row = jnp.where(last, 0.0, n_ref[0:1, :])
        prev, nxt = _shift_rows(x, prev_row, next_row)
        return x + mu_ref[0:1, :] * (prev - x) + mu_ref[1:2, :] * (nxt - x)

    cs = tshift(cs_ref, csp_ref, csn_ref, mucs_ref)
    co = tshift(co_ref, cop_ref, con_ref, muco_ref)
    k = cs[:, :D_C]
    v = cs[:, D_C:2 * D_C]
    wd = cs[:, 2 * D_C:2 * D_C + 2 * LORA_W]
    ad = cs[:, 2 * D_C + 2 * LORA_W:]
    r = co[:, :D_C]
    gd = co[:, D_C:]
    bd = bd_ref[...]

    zw = _dot3(jnp.tanh(wd), w2_ref[...])
    za = _dot3(ad, a2_ref[...])
    kx = k * kk_ref[...]
    ss = _dot_exact_rhs(kx * kx, bd)
    kn = kx / jnp.maximum(jnp.sqrt(ss), 1e-12)
    kd_sum = jnp.zeros_like(k)
    for d in range(2):
        sl = slice(d * D_C, (d + 1) * D_C)
        z = -(w0_ref[d:d + 1, :] + zw[:, sl])
        softplus = jnp.maximum(z, 0.0) + jnp.log(1.0 + jnp.exp(-jnp.abs(z)))
        lw_ref[d] = -jnp.exp(-softplus - 0.5)
        a = _sigmoid(a0_ref[d:d + 1, :] + za[:, sl])
        kd = k * (1.0 + (a - 1.0) * ka_ref[...])
        kd_ref[d] = kd
        bb_ref[d] = kn * a
        kd_sum = kd_sum + kd
    v_ref[...] = v
    kn_ref[...] = kn
    r_ref[...] = r
    bonus_ref[...] = _dot_exact_rhs(r * kd_sum * rk_ref[...], bd) * v
    g_ref[...] = _dot3(_sigmoid(gd), g2_ref[...])


def _rwkv_prep(cs, co, lp, st):
    n = st.n_rows
    row = lambda i: (i, 0)
    full = lambda shape: pl.BlockSpec(shape, lambda i: (0,) * len(shape))
    csp, csn = _halo_specs(D_CS, 0, n)
    cop, con = _halo_specs(D_CO, 0, n)
    dir_out = pl.BlockSpec((2, TM, D_C), lambda i: (0, i, 0))
    tok_out = pl.BlockSpec((TM, D_C), row)
    return pl.pallas_call(
        functools.partial(_rwkv_prep_kernel, st=st),
        grid=(st.tiles,),
        in_specs=[pl.BlockSpec((TM, D_CS), row), csp, csn, pl.BlockSpec((TM, D_CO), row), cop, con,
                  full((2, D_CS)), full((2, D_CO)), full((2, D_C)), full((2 * LORA_W, 2 * D_C)),
                  full((2, D_C)), full((2 * LORA_A, 2 * D_C)), full((LORA_G, D_C)),
                  full((1, D_C)), full((1, D_C)), full((1, D_C)), full((D_C, D_C))],
        out_specs=[dir_out, dir_out, dir_out, tok_out, tok_out, tok_out, tok_out, tok_out],
        out_shape=[jax.ShapeDtypeStruct((2, n, D_C), F32)] * 3 + [jax.ShapeDtypeStruct((n, D_C), F32)] * 5,
        compiler_params=_cparams(("arbitrary",)),
        name="rwkv_prep",
    )(cs, cs, cs, co, co, co, lp["mu_cs"], lp["mu_co"], lp["w0"], lp["w2bd"], lp["a0"], lp["a2bd"],
      lp["g2"], lp["k_k"], lp["k_a"], lp["r_k"], lp["head_ones"])


def _rwkv_chunk_kernel(lw_ref, kd_ref, bb_ref, v_ref, kn_ref, r_ref, qh_ref, y0_ref, g_ref, j_ref):
    d = pl.program_id(0)
    L = CHUNK
    ti = lax.broadcasted_iota(jnp.int32, (L, L), 0)
    si = lax.broadcasted_iota(jnp.int32, (L, L), 1)
    sgn = jnp.where(d == 0, 1, -1)
    incl = jnp.where((ti - si) * sgn >= 0, 1.0, 0.0).astype(BF16)
    ri = lax.broadcasted_iota(jnp.int32, (PAIR, PAIR), 0)
    ci = lax.broadcasted_iota(jnp.int32, (PAIR, PAIR), 1)
    same_head = (ri // HS_C) == (ci // HS_C)
    lag = jnp.where(same_head, (ri % L - ci % L) * sgn, -1)
    m_strict = lag > 0
    m_incl = lag >= 0
    eye = (ri == ci).astype(F32)
    lane = lax.broadcasted_iota(jnp.int32, (L, D_C), 1)
    head0 = (lane % PAIR) < HS_C

    def stack(x):
        return jnp.concatenate([jnp.where(head0, x, 0.0), jnp.where(head0, 0.0, x)], axis=0)

    def fold(x):
        return x[:L] + x[L:]

    pairs = range(N_PAIR)
    sls = [slice(p * PAIR, (p + 1) * PAIR) for p in pairs]
    H = PAIR
    units = [(c, p) for c in range(CPS) for p in pairs]

    def wide(c):
        rs = slice(c * L, (c + 1) * L)
        lw, kd, bb = lw_ref[0, rs, :], kd_ref[0, rs, :], bb_ref[0, rs, :]
        v, kn, r = v_ref[rs, :], kn_ref[rs, :], r_ref[rs, :]
        cum = _dot_exact_rhs_lhs(incl, lw)
        tot = jnp.sum(lw, axis=0, keepdims=True)
        e_inv = jnp.exp(-cum)
        e_end = jnp.exp(tot - cum)
        xr_f = stack(r * jnp.exp(cum))
        a_t = -bb * e_inv
        k_t = kd * e_inv
        return dict(
            e_tot=jnp.exp(tot), xr_f=xr_f,
            rows=jnp.concatenate([stack(kn * jnp.exp(cum - lw)), xr_f], axis=0).astype(BF16),
            cols=jnp.concatenate([a_t, a_t, k_t, k_t], axis=0).astype(BF16),
            vs=stack(v).astype(BF16),
            ends=jnp.concatenate([-bb * e_end, kd * e_end], axis=0).astype(BF16),
            v_b=v.astype(BF16))

    ch = [wide(c) for c in range(CPS)]
    big = [_dot_nt(ch[c]["rows"][:, sls[p]], ch[c]["cols"][:, sls[p]]) for c, p in units]
    n_mat = [jnp.where(m_strict, x[:H, :H], 0.0) for x in big]
    m_bk = [jnp.where(m_strict, x[:H, H:], 0.0).astype(BF16) for x in big]
    m_ra = [jnp.where(m_incl, x[H:, :H], 0.0).astype(BF16) for x in big]
    m_rk = [jnp.where(m_incl, x[H:, H:], 0.0).astype(BF16) for x in big]
    n_u = range(len(units))
    u1 = [_dot(m_bk[u], ch[c]["vs"][:, sls[p]]) for u, (c, p) in enumerate(units)]
    t_mat = [eye + x for x in n_mat]
    n_b = [x.astype(BF16) for x in n_mat]
    n_pow = [_dot(x, x).astype(BF16) for x in n_b]
    span = 2
    while span < L // 2:
        res = [_dot(jnp.concatenate([n_pow[u], t_mat[u].astype(BF16)], axis=0), n_pow[u]) for u in n_u]
        n_pow = [x[:H].astype(BF16) for x in res]
        t_mat = [t_mat[u] + res[u][H:] for u in n_u]
        span *= 2
    t_mat = [t_mat[u] + _dot(t_mat[u].astype(BF16), n_pow[u]) for u in n_u]
    wu = [_dot(t_mat[u].astype(BF16), jnp.concatenate([ch[c]["rows"][:H, sls[p]], u1[u].astype(BF16)], axis=1))
          for u, (c, p) in enumerate(units)]
    qy = [_dot(m_ra[u], wu[u].astype(BF16)) for u in n_u]
    y0x = [_dot(m_rk[u], ch[c]["vs"][:, sls[p]]) for u, (c, p) in enumerate(units)]
    zeros = jnp.zeros((L, H), BF16)
    gj = [_dot_tn(ch[c]["ends"][:, sls[p]],
                  jnp.concatenate([fold(wu[u]).astype(BF16),
                                   jnp.concatenate([zeros, ch[c]["v_b"][:, sls[p]]], axis=1)], axis=0))
          for u, (c, p) in enumerate(units)]
    for u, (c, p) in enumerate(units):
        pos = jnp.where(d == 0, c, CPS - 1 - c)
        qh_ref[0, pos, 0, :, sls[p]] = fold(ch[c]["xr_f"][:, sls[p]] + qy[u][:, :H])
        y0_ref[0, pos, 0, :, sls[p]] = fold(qy[u][:, H:] + y0x[u])
        g_ref[0, pos, 0, p] = jnp.where(same_head, gj[u][:, :H], 0.0) + eye * ch[c]["e_tot"][:, sls[p]]
        j_ref[0, pos, 0, p] = jnp.where(same_head, gj[u][:, H:], 0.0)


def _dot_exact_rhs_lhs(mask_bf16, x):
    x1 = x.astype(BF16)
    r1 = x - x1.astype(F32)
    x2 = r1.astype(BF16)
    x3 = (r1 - x2.astype(F32)).astype(BF16)
    return _dot(mask_bf16, x1) + (_dot(mask_bf16, x2) + _dot(mask_bf16, x3))


def _scan_geometry(st):
    nc_ctx = st.Lc // CHUNK
    nc_lat = st.T // CHUNK
    ns = nc_ctx + nc_lat
    ctx_base = st.n_lat_rows // CHUNK

    def row_block(d, b, s):
        in_ctx = s < nc_ctx
        j_ctx = jnp.where(d == 0, s, nc_ctx - 1 - s)
        j_lat = jnp.where(d == 0, s - nc_ctx, nc_lat - 1 - (s - nc_ctx))
        return jnp.where(in_ctx, ctx_base + b * nc_ctx + j_ctx, b * nc_lat + j_lat)

    return nc_ctx, nc_lat, ns, row_block


def _rwkv_chunks(lw, kd, bb, v, kn, r, st):
    B = st.B
    nc_ctx, nc_lat, ns, row_block = _scan_geometry(st)
    assert nc_ctx % CPS == 0 and nc_lat % CPS == 0
    rows = CPS * CHUNK
    blk = lambda d, b, s: row_block(d, b, s * CPS) // CPS
    dir_in = pl.BlockSpec((1, rows, D_C), lambda d, b, s: (d, blk(d, b, s), 0))
    tok_in = pl.BlockSpec((rows, D_C), lambda d, b, s: (blk(d, b, s), 0))
    row_out = pl.BlockSpec((1, CPS, 1, CHUNK, D_C), lambda d, b, s: (d, s, b, 0, 0))
    mat_out = pl.BlockSpec((1, CPS, 1, N_PAIR, PAIR, PAIR), lambda d, b, s: (d, s, b, 0, 0, 0))
    return pl.pallas_call(
        _rwkv_chunk_kernel,
        grid=(2, B, ns // CPS),
        in_specs=[dir_in, dir_in, dir_in, tok_in, tok_in, tok_in],
        out_specs=[row_out, row_out, mat_out, mat_out],
        out_shape=[jax.ShapeDtypeStruct((2, ns, B, CHUNK, D_C), F32)] * 2
        + [jax.ShapeDtypeStruct((2, ns, B, N_PAIR, PAIR, PAIR), F32)] * 2,
        compiler_params=_cparams(("arbitrary", "arbitrary", "arbitrary")),
        name="rwkv_chunks",
    )(lw, kd, bb, v, kn, r)


def _rwkv_carry_kernel(qh_ref, y0_ref, g_ref, j_ref, y_ref, h_ref):
    s = pl.program_id(0)

    @pl.when(s == 0)
    def _():
        h_ref[...] = jnp.zeros_like(h_ref)

    nb = qh_ref.shape[2]
    for d in range(2):
        for b in range(nb):
            for p in range(N_PAIR):
                sl = slice(p * PAIR, (p + 1) * PAIR)
                h = h_ref[d, b, p]
                lhs = jnp.concatenate([qh_ref[d, 0, b, :, sl], g_ref[d, 0, b, p]], axis=0)
                res = _dot3(lhs, h)
                y_ref[d, 0, b, :, sl] = res[:CHUNK] + y0_ref[d, 0, b, :, sl]
                h_ref[d, b, p] = res[CHUNK:] + j_ref[d, 0, b, p]


def _rwkv_carry(qh, y0, g, j, st):
    B = st.B
    ns = qh.shape[1]
    rows = pl.BlockSpec((2, 1, B, CHUNK, D_C), lambda s: (0, s, 0, 0, 0))
    mats = pl.BlockSpec((2, 1, B, N_PAIR, PAIR, PAIR), lambda s: (0, s, 0, 0, 0, 0))
    return pl.pallas_call(
        _rwkv_carry_kernel,
        grid=(ns,),
        in_specs=[rows, rows, mats, mats],
        out_specs=rows,
        out_shape=jax.ShapeDtypeStruct((2, ns, B, CHUNK, D_C), F32),
        scratch_shapes=[pltpu.VMEM((2, B, N_PAIR, PAIR, PAIR), F32)],
        compiler_params=_cparams(("arbitrary",)),
        name="rwkv_carry",
    )(qh, y0, g, j)


def _mix_out_kernel(oa_ref, of_ref, yf_ref, yb_ref, bonus_ref, rg_ref, gate_ref, x_ref, mod_ref, gnw_ref, gnb_ref,
                    bd_ref, wpa_ref, wpf_ref, wpc_ref, wo_ref, lng_ref, lnb_ref, x1_ref, h2_ref, *, alpha):
    n = TM // CHUNK
    y = jnp.concatenate([yf_ref[0, c, 0] + yb_ref[0, n - 1 - c, 0] for c in range(n)], axis=0)
    bd = bd_ref[...]
    mu = _dot_exact_rhs(y, bd) * (1.0 / HS_C)
    yc = y - mu
    var = _dot_exact_rhs(yc * yc, bd) * (1.0 / HS_C)
    yn = yc * lax.rsqrt(var + GN_EPS) * gnw_ref[...] + gnb_ref[...]
    o_c = ((yn + bonus_ref[...]) * rg_ref[...]).astype(BF16)
    g0 = gate_ref[:, 0:D_MODEL].astype(F32)
    g1 = gate_ref[:, D_MODEL:2 * D_MODEL].astype(F32)
    g2 = gate_ref[:, 2 * D_MODEL:].astype(F32)
    y = g0 * _dot(oa_ref[...], wpa_ref[...]) + g1 * _dot(of_ref[...], wpf_ref[...]) + g2 * _dot(o_c, wpc_ref[...])
    mix = _dot(y.astype(BF16), wo_ref[...])
    z = alpha * x_ref[...] + mod_ref[0, 2:3, :] * mix
    x1 = _ln_plain(z) * lng_ref[...] + lnb_ref[...]
    x1_ref[...] = x1
    h2_ref[...] = (_ln_plain(x1) * (1.0 + mod_ref[0, 4:5, :]) + mod_ref[0, 3:4, :]).astype(BF16)


def _mix_out(o_a, o_f, y, bonus, rg, gate, x_all, mod, lp, st, n_tiles, alpha):
    nc_ctx, nc_lat, _, _ = _scan_geometry(st)
    n = TM // CHUNK
    assert nc_ctx % n == 0

    def seq_of(i):
        lat = i < st.lat_tiles
        b = jnp.where(lat, i // st.tpb, (i - st.lat_tiles) // st.tpc)
        j = jnp.where(lat, i % st.tpb, (i - st.lat_tiles) % st.tpc)
        return lat, b, j

    def fwd(i):
        lat, b, j = seq_of(i)
        return (0, jnp.where(lat, nc_ctx // n + j, j), b, 0, 0)

    def bwd(i):
        lat, b, j = seq_of(i)
        return (1, jnp.where(lat, nc_ctx // n + (st.tpb - 1 - j), st.tpc - 1 - j), b, 0, 0)

    row = lambda i: (i, 0)
    full = lambda shape: pl.BlockSpec(shape, lambda i: (0,) * len(shape))
    return pl.pallas_call(
        functools.partial(_mix_out_kernel, alpha=alpha),
        grid=(n_tiles,),
        in_specs=[pl.BlockSpec((TM, D_A), row), pl.BlockSpec((TM, D_F), row),
                  pl.BlockSpec((1, n, 1, CHUNK, D_C), fwd), pl.BlockSpec((1, n, 1, CHUNK, D_C), bwd),
                  pl.BlockSpec((TM, D_C), row), pl.BlockSpec((TM, D_C), row),
                  pl.BlockSpec((TM, N_BRANCH * D_MODEL), row), pl.BlockSpec((TM, D_MODEL), row),
                  pl.BlockSpec((1, 6, D_MODEL), lambda i: (st.mod_index(i), 0, 0)),
                  full((1, D_C)), full((1, D_C)), full((D_C, D_C)),
                  full((D_A, D_MODEL)), full((D_F, D_MODEL)), full((D_C, D_MODEL)), full((D_MODEL, D_MODEL)),
                  full((1, D_MODEL)), full((1, D_MODEL))],
        out_specs=[pl.BlockSpec((TM, D_MODEL), row), pl.BlockSpec((TM, D_MODEL), row)],
        out_shape=[jax.ShapeDtypeStruct((n_tiles * TM, D_MODEL), F32),
                   jax.ShapeDtypeStruct((n_tiles * TM, D_MODEL), BF16)],
        compiler_params=_cparams(("arbitrary",)),
        name="mix_out",
    )(o_a, o_f, y, y, bonus, rg, gate, x_all, mod, lp["ln_w"], lp["ln_b"], lp["head_ones"],
      lp["w_pa"], lp["w_pf"], lp["w_pc"], lp["w_o"], lp["ln1_g"], lp["ln1_b"])


def _ffn_down_kernel(u_ref, g_ref, gp_ref, gn_ref, x_ref, mod_ref, cw_ref, cb_ref, wd_ref, lng_ref, lnb_ref,
                     o_ref, *, st, alpha):
    i = pl.program_id(0)
    first, last = st.seq_pos(i)
    g = g_ref[...].astype(F32)
    prev_row = jnp.where(first, 0.0, gp_ref[HALO - 1:HALO, :].astype(F32))
    next_row = jnp.where(last, 0.0, gn_ref[0:1, :].astype(F32))
    prev, nxt = _shift_rows(g, prev_row, next_row)
    gc = cw_ref[0:1, :] * prev + cw_ref[1:2, :] * g + cw_ref[2:3, :] * nxt + cb_ref[...]
    act = 0.5 * gc * (1.0 + lax.erf(gc * (2.0 ** -0.5)))
    a = (act * u_ref[...].astype(F32)).astype(BF16)
    z = alpha * x_ref[...] + mod_ref[0, 5:6, :] * _dot(a, wd_ref[...])
    o_ref[...] = _ln_plain(z) * lng_ref[...] + lnb_ref[...]


def _ffn_down(ug, x1, mod, lp, st, n_tiles, alpha):
    row = lambda i: (i, 0)
    full = lambda shape: pl.BlockSpec(shape, lambda i: (0,) * len(shape))
    gp, gn = _halo_specs(D_FF, 1, ug.shape[0])
    return pl.pallas_call(
        functools.partial(_ffn_down_kernel, st=st, alpha=alpha),
        grid=(n_tiles,),
        in_specs=[pl.BlockSpec((TM, D_FF), lambda i: (i, 0)), pl.BlockSpec((TM, D_FF), lambda i: (i, 1)), gp, gn,
                  pl.BlockSpec((TM, D_MODEL), row),
                  pl.BlockSpec((1, 6, D_MODEL), lambda i: (st.mod_index(i), 0, 0)),
                  full((3, D_FF)), full((1, D_FF)), full((D_FF, D_MODEL)), full((1, D_MODEL)), full((1, D_MODEL))],
        out_specs=pl.BlockSpec((TM, D_MODEL), row),
        out_shape=jax.ShapeDtypeStruct((n_tiles * TM, D_MODEL), F32),
        compiler_params=_cparams(("arbitrary",)),
        name="ffn_down",
    )(ug, ug, ug, ug, x1, mod, lp["ffn_conv"], lp["ffn_conv_b"], lp["ffn_down"], lp["ln2_g"], lp["ln2_b"])


def _rope_tables(st):
    T = st.T
    pos = jnp.arange(T)
    rowp = (pos // GRID_W).astype(F32)
    colp = (pos % GRID_W).astype(F32)
    n_freq = DH_A // 4
    inv = ROPE_BASE ** (-jnp.arange(n_freq, dtype=F32) / n_freq)
    ar, ac = rowp[:, None] * inv, colp[:, None] * inv
    cos64 = jnp.concatenate([jnp.cos(ar), jnp.cos(ar), jnp.cos(ac), jnp.cos(ac)], axis=1)
    sin64 = jnp.concatenate([-jnp.sin(ar), jnp.sin(ar), -jnp.sin(ac), jnp.sin(ac)], axis=1)
    cos_q = jnp.concatenate([cos64, cos64], axis=1)
    sin_q = jnp.concatenate([sin64, sin64], axis=1)
    n_ctx = st.B * st.Lc
    cos_k = jnp.concatenate([jnp.tile(cos_q, (st.B, 1)), jnp.ones((n_ctx, LANES), F32)], axis=0)
    sin_k = jnp.concatenate([jnp.tile(sin_q, (st.B, 1)), jnp.zeros((n_ctx, LANES), F32)], axis=0)
    return cos_q, sin_q, cos_k, sin_k


def _block_diag2(m):
    z = jnp.zeros_like(m[0])
    return jnp.concatenate([jnp.concatenate([m[0], z], axis=1), jnp.concatenate([z, m[1]], axis=1)], axis=0)


def kernel(x, c, ctx, c_ctx, ada_w, ada_b, w_in, lam_qk, subln_g, tshift_mu, rw_w0, rw_w2, rw_a0, rw_a2, rw_g2, rw_kk, rw_ka, rw_rk, rw_lnw, rw_lnb, w_pa, w_pf, w_pc, w_o, ln1_g, ln1_b, ffn_up, ffn_conv, ffn_conv_b, ffn_down, ln2_g, ln2_b):
    B, T, D = x.shape
    Lc = ctx.shape[1]
    depth = w_in.shape[0]
    st = _Stream(B, T, Lc)
    alpha = (2.0 * depth) ** 0.25

    cos_q, sin_q, cos_k, sin_k = _rope_tables(st)
    dft_lat = _dft_tables(T)
    dft_ctx = _dft_tables(Lc)
    kf = jnp.arange(F_GROUP_W, dtype=jnp.int32)
    ang = ((kf[:, None] * kf[None, :]) % F_GROUP_W).astype(F32) * (2.0 * math.pi / F_GROUP_W)
    dft_ch = jnp.concatenate([jnp.cos(ang), jnp.sin(ang)], axis=1).astype(BF16)
    hi = jnp.arange(D_C) // HS_C
    head_ones = (hi[:, None] == hi[None, :]).astype(BF16)
    c_all = jnp.concatenate([c, c_ctx[None], jnp.zeros((8 - B - 1, D), F32)], axis=0)

    x_all = jnp.concatenate([x.reshape(B * T, D), ctx.reshape(B * Lc, D)], axis=0)
    for i in range(depth):
        last = i == depth - 1
        lam_init = 0.8 - 0.6 * math.exp(-0.3 * i)
        lq = lam_qk[i].astype(F32)
        lam = (jnp.exp(jnp.sum(lq[0] * lq[1])) - jnp.exp(jnp.sum(lq[2] * lq[3])) + lam_init).reshape(1, 1)
        wi = w_in[i].astype(BF16)
        lp = {
            "mu_cs": tshift_mu[i][:, :D_CS], "mu_co": tshift_mu[i][:, D_CS:],
            "w0": rw_w0[i], "w2bd": _block_diag2(rw_w2[i]), "a0": rw_a0[i], "a2bd": _block_diag2(rw_a2[i]),
            "g2": rw_g2[i], "k_k": rw_kk[i].reshape(1, D_C), "k_a": rw_ka[i].reshape(1, D_C),
            "r_k": rw_rk[i].reshape(1, D_C), "ln_w": rw_lnw[i].reshape(1, D_C), "ln_b": rw_lnb[i].reshape(1, D_C),
            "head_ones": head_ones,
            "w_pa": w_pa[i].astype(BF16), "w_pf": w_pf[i].astype(BF16), "w_pc": w_pc[i].astype(BF16),
            "w_o": w_o[i].astype(BF16), "ln1_g": ln1_g[i].reshape(1, D), "ln1_b": ln1_b[i].reshape(1, D),
            "ffn_conv": ffn_conv[i], "ffn_conv_b": ffn_conv_b[i].reshape(1, D_FF),
            "ffn_down": ffn_down[i].astype(BF16), "ln2_g": ln2_g[i].reshape(1, D), "ln2_b": ln2_b[i].reshape(1, D),
        }
        n_out = st.lat_tiles if last else st.tiles

        mod = _adaln(c_all, ada_w[i], ada_b[i]).reshape(8, 6, D)
        q, k, v, z = _proj(x_all, wi[:, :OFF_CS], ((D_A, 512, "plain", F32), (D_A, 512, "rope", BF16),
                                                    (D_A, 512, "plain", BF16), (D_F, 512, "dft", BF16)),
                           (cos_k, sin_k, dft_ch), mod, st)
        cs, co, gate = _proj(x_all, wi[:, OFF_CS:], ((D_CS, D_CS // 2, "plain", F32), (D_CO, D_CO, "plain", F32),
                                                       (N_BRANCH * D_MODEL, 512, "sigmoid", BF16)), (), mod, st)

        subln = subln_g[i].reshape(1, DV_A)
        o_a = _attention(lam, q, k, v, cos_q, sin_q, subln, st, 1.0 - lam_init, True)
        o_f = _dft2(dft_lat, z, B, T, 0)
        if not last:
            o_a = jnp.concatenate([o_a, _attention(lam, q, k, v, cos_q, sin_q, subln, st, 1.0 - lam_init, False)], axis=0)
            o_f = jnp.concatenate([o_f, _dft2(dft_ctx, z, B, Lc, st.n_lat_rows)], axis=0)

        lw, kd, bb, rv, kn, rr, bonus, rg = _rwkv_prep(cs, co, lp, st)
        qh, y0, gm, jm = _rwkv_chunks(lw, kd, bb, rv, kn, rr, st)
        y = _rwkv_carry(qh, y0, gm, jm, st)
        x1, h2 = _mix_out(o_a, o_f, y, bonus, rg, gate, x_all, mod, lp, st, n_out, alpha)
        (ug,) = _proj(h2, ffn_up[i].astype(BF16), ((2 * D_FF, 512, "plain", BF16),))
        x_all = _ffn_down(ug, x1, mod, lp, st, n_out, alpha)
    return x_all[:B * T].reshape(B, T, D)
```

```python
import functools
import math

import jax
import jax.numpy as jnp
from jax import lax
from jax.experimental import pallas as pl
from jax.experimental.pallas import tpu as pltpu

F32 = jnp.float32
BF16 = jnp.bfloat16

D_MODEL = 1024
DEPTH = 2
GRID_W = 64
HA = 8
DH_A = 64
DV_A = 2 * DH_A
D_A = HA * DV_A
ROPE_BASE = 10000.0
F_GROUPS = 4
F_GROUP_W = 128
D_F = F_GROUPS * F_GROUP_W
HC = 8
HS_C = 64
D_C = HC * HS_C
LORA_W = 64
LORA_A = 64
LORA_G = 128
D_CS = 2 * D_C + 2 * LORA_W + 2 * LORA_A
D_CO = D_C + LORA_G
N_BRANCH = 3
D_FF = 2816
LN_EPS = 1e-5
GN_EPS = HS_C * 1e-5
HEAD_NORM_EPS = 1e-5

OFF_Q = 0
OFF_K = OFF_Q + D_A
OFF_V = OFF_K + D_A
OFF_F = OFF_V + D_A
OFF_CS = OFF_F + D_F
OFF_CO = OFF_CS + D_CS
OFF_GATE = OFF_CO + D_CO
D_IN = OFF_GATE + N_BRANCH * D_MODEL

LANES = 128
HALO = 16
TM = 256
TMP = 512
CHUNK = 64
PAIR = 2 * HS_C
CPS = 4
N_PAIR = D_C // PAIR
VMEM_LIMIT = 56 * 1024 * 1024


def _cparams(sem):
    return pltpu.CompilerParams(dimension_semantics=sem, vmem_limit_bytes=VMEM_LIMIT)


def _split2(x):
    hi = x.astype(BF16)
    lo = (x - hi.astype(F32)).astype(BF16)
    return hi, lo


def _dot(a, b, dims=(((1,), (0,)), ((), ()))):
    return lax.dot_general(a, b, dims, preferred_element_type=F32)


def _dot_nt(a, b):
    return lax.dot_general(a, b, (((1,), (1,)), ((), ())), preferred_element_type=F32)


def _dot_tn(a, b):
    return lax.dot_general(a, b, (((0,), (0,)), ((), ())), preferred_element_type=F32)


def _dotb(a, b):
    return _dot(a.astype(BF16), b.astype(BF16))


def _dot3(a, b):
    ah, al = _split2(a)
    bh, bl = _split2(b)
    return _dot(ah, bh) + (_dot(ah, bl) + _dot(al, bh))


def _dot3w(a, w_ref):
    ah, al = _split2(a)
    return _dot(ah, w_ref[0]) + (_dot(ah, w_ref[1]) + _dot(al, w_ref[0]))


def _dot_exact_rhs(a, b_bf16):
    a1, a2 = _split2(a)
    return _dot(a1, b_bf16) + _dot(a2, b_bf16)


def _ln_plain(x):
    mu = jnp.mean(x, axis=-1, keepdims=True)
    xc = x - mu
    var = jnp.mean(xc * xc, axis=-1, keepdims=True)
    return xc * lax.rsqrt(var + LN_EPS)


def _sigmoid(x):
    return 1.0 / (1.0 + jnp.exp(-x))


def _swap16(x):
    n = x.shape[-1]
    lane = lax.broadcasted_iota(jnp.int32, x.shape, x.ndim - 1)
    up = pltpu.roll(x, n - 16, axis=x.ndim - 1)
    dn = pltpu.roll(x, 16, axis=x.ndim - 1)
    return jnp.where((lane % 32) < 16, up, dn)


def _shift_rows(x, prev_row, next_row):
    n = x.shape[0]
    row = lax.broadcasted_iota(jnp.int32, x.shape, 0)
    prev = jnp.where(row == 0, prev_row, pltpu.roll(x, 1, axis=0))
    nxt = jnp.where(row == n - 1, next_row, pltpu.roll(x, n - 1, axis=0))
    return prev, nxt


class _Stream:
    def __init__(self, B, T, Lc):
        assert T % TM == 0 and Lc % TM == 0 and (B * T) % Lc == 0
        self.B, self.T, self.Lc = B, T, Lc
        self.n_lat_rows = B * T
        self.n_rows = B * T + B * Lc
        self.lat_tiles = B * T // TM
        self.tiles = self.n_rows // TM
        self.tpb = T // TM
        self.tpc = Lc // TM

    def mod_index(self, i):
        return jnp.where(i < self.lat_tiles, i // self.tpb, self.B)

    def seq_pos(self, i):
        j = jnp.where(i < self.lat_tiles, i % self.tpb, (i - self.lat_tiles) % self.tpc)
        n = jnp.where(i < self.lat_tiles, self.tpb, self.tpc)
        return j == 0, j == n - 1


def _halo_specs(width, col_block, n_rows):
    per = TM // HALO
    last = n_rows // HALO - 1
    prev = pl.BlockSpec((HALO, width), lambda i: (jnp.maximum(i * per - 1, 0), col_block))
    nxt = pl.BlockSpec((HALO, width), lambda i: (jnp.minimum((i + 1) * per, last), col_block))
    return prev, nxt


def _adaln_kernel(c_ref, w_ref, b_ref, o_ref):
    c = c_ref[...]
    s = c * _sigmoid(c)
    o_ref[...] = _dot3(s, w_ref[...]) + b_ref[...]


def _adaln(c_all, w, b):
    n = w.shape[1]
    tn = 1536
    return pl.pallas_call(
        _adaln_kernel,
        grid=(n // tn,),
        in_specs=[pl.BlockSpec((8, D_MODEL), lambda j: (0, 0)),
                  pl.BlockSpec((D_MODEL, tn), lambda j: (0, j)),
                  pl.BlockSpec((1, tn), lambda j: (0, j))],
        out_specs=pl.BlockSpec((8, tn), lambda j: (0, j)),
        out_shape=jax.ShapeDtypeStruct((8, n), F32),
        compiler_params=_cparams(("arbitrary",)),
        name="adaln",
    )(c_all, w, b.reshape(1, n))


def _proj_kernel(a_ref, w_ref, *rest, groups, has_mod, n_tables, lat_tiles):
    if has_mod:
        mod_ref, rest = rest[0], rest[1:]
        a = (_ln_plain(a_ref[...]) * (1.0 + mod_ref[0, 1:2, :]) + mod_ref[0, 0:1, :]).astype(BF16)
    else:
        a = a_ref[...]
    tables, outs = rest[:n_tables], rest[n_tables:]
    off = 0
    for (width, chunk, epi), o_ref in zip(groups, outs):
        for c0 in range(0, width, chunk):
            acc = _dot(a, w_ref[:, off + c0:off + c0 + chunk])
            if epi == "rope":
                reps = chunk // LANES
                lat = pl.program_id(0) < lat_tiles
                cos = jnp.where(lat, jnp.concatenate([tables[0][...]] * reps, axis=1), 1.0)
                sin = jnp.where(lat, jnp.concatenate([tables[1][...]] * reps, axis=1), 0.0)
                acc = acc * cos + _swap16(acc) * sin
            elif epi == "sigmoid":
                acc = _sigmoid(acc)
            if epi == "dft":
                fb = acc.astype(BF16)
                for g in range(chunk // F_GROUP_W):
                    sl = slice(c0 + g * F_GROUP_W, c0 + (g + 1) * F_GROUP_W)
                    r = _dot(fb[:, g * F_GROUP_W:(g + 1) * F_GROUP_W], tables[2][...])
                    o_ref[0, :, sl] = r[:, :F_GROUP_W].astype(o_ref.dtype)
                    o_ref[1, :, sl] = r[:, F_GROUP_W:].astype(o_ref.dtype)
            else:
                o_ref[:, c0:c0 + chunk] = acc.astype(o_ref.dtype)
        off += width


def _proj(h, w, groups, tables=(), mod=None, st=None):
    m, k = h.shape
    n = w.shape[1]
    assert m % TMP == 0 and sum(g[0] for g in groups) == n and all(g[0] % g[1] == 0 for g in groups)
    row = lambda i: (i, 0)
    specs = [pl.BlockSpec((TMP, k), row), pl.BlockSpec((k, n), lambda i: (0, 0))]
    args = [h, w]
    lat_tiles, per_seq = (st.n_lat_rows // TMP, st.T // TMP) if st is not None else (0, 1)
    if mod is not None:
        specs.append(pl.BlockSpec((1, 6, D_MODEL), lambda i: (jnp.where(i < lat_tiles, i // per_seq, st.B), 0, 0)))
        args.append(mod)
    for t in tables:
        if st is not None and t.shape == (st.T, LANES):
            specs.append(pl.BlockSpec((TMP, LANES), lambda i: (jnp.where(i < lat_tiles, i % per_seq, 0), 0)))
        else:
            specs.append(pl.BlockSpec(t.shape, lambda i: (0, 0)))
    out_specs = [pl.BlockSpec((2, TMP, g[0]), lambda i: (0, i, 0)) if g[2] == "dft" else pl.BlockSpec((TMP, g[0]), row)
                 for g in groups]
    out_shape = [jax.ShapeDtypeStruct((2, m, g[0]) if g[2] == "dft" else (m, g[0]), g[3]) for g in groups]
    return pl.pallas_call(
        functools.partial(_proj_kernel, groups=tuple(g[:3] for g in groups), has_mod=mod is not None,
                          n_tables=len(tables), lat_tiles=lat_tiles),
        grid=(m // TMP,),
        in_specs=specs,
        out_specs=out_specs,
        out_shape=out_shape,
        compiler_params=_cparams(("arbitrary",)),
        name="proj_" + "_".join(g[2] for g in groups),
    )(*args, *tables)


TQ = 512
TK = 512
AHEAD = 1


def _attn_kernel(lam_ref, q_ref, cos_ref, sin_ref, g_ref, kc_ref, vc_ref, *rest, n_kblk, out_scale):
    if n_kblk:
        kl_ref, vl_ref, o_ref = rest
    else:
        (o_ref,) = rest
    tq = q_ref.shape[0]
    lane = lax.broadcasted_iota(jnp.int32, (tq, LANES), 1)
    first = lane < DH_A
    q = q_ref[...] * (DH_A ** -0.5 * math.log2(math.e))

    def stack(x):
        return jnp.concatenate([jnp.where(first, x, 0.0), jnp.where(first, 0.0, x)], axis=0).astype(BF16)

    def update(s, v, m, acc):
        m_new = jnp.maximum(m, jnp.max(s, axis=-1, keepdims=True))
        p = jnp.exp2((s - m_new).astype(BF16))
        pv = _dot(p, jnp.concatenate([v, jnp.ones_like(v)], axis=1))
        return m_new, acc * jnp.exp2(m - m_new) + pv

    carry = (jnp.full((2 * tq, 1), -1e30, F32), jnp.zeros((2 * tq, 2 * DV_A), F32))
    pending = [(_dot_nt(stack(q), kc_ref[...]), vc_ref[...])]
    if n_kblk:
        q_rot = stack(q * cos_ref[...] + _swap16(q) * sin_ref[...])
        for j in range(n_kblk):
            pending.append((_dot_nt(q_rot, kl_ref[j * TK:(j + 1) * TK, :]), vl_ref[j * TK:(j + 1) * TK, :]))
            if len(pending) > AHEAD:
                carry = update(*pending.pop(0), *carry)
    for s, v in pending:
        carry = update(s, v, *carry)
    _, acc = carry
    o = acc[:, :DV_A] / acc[:, DV_A:DV_A + 1]
    o = o[:tq] - lam_ref[0, 0] * o[tq:]
    o = o * lax.rsqrt(jnp.mean(o * o, axis=-1, keepdims=True) + HEAD_NORM_EPS)
    o_ref[...] = (o * (g_ref[...] * out_scale)).astype(o_ref.dtype)


def _attention(lam, q, k, v, cos_q, sin_q, subln, st, out_scale, latent):
    B, T, Lc = st.B, st.T, st.Lc
    ctx_blk0 = B * T // Lc
    tq = min(TQ, T if latent else Lc)
    n_q = (T if latent else Lc) // tq
    q_row0 = 0 if latent else B * T // tq
    n_kblk = T // TK if latent else 0
    in_specs = [
        pl.BlockSpec(memory_space=pltpu.SMEM),
        pl.BlockSpec((tq, LANES), lambda b, h, i: (q_row0 + b * n_q + i, h)),
        pl.BlockSpec((tq, LANES), lambda b, h, i: (i if latent else 0, 0)),
        pl.BlockSpec((tq, LANES), lambda b, h, i: (i if latent else 0, 0)),
        pl.BlockSpec((1, LANES), lambda b, h, i: (0, 0)),
        pl.BlockSpec((Lc, LANES), lambda b, h, i: (ctx_blk0 + b, h)),
        pl.BlockSpec((Lc, LANES), lambda b, h, i: (ctx_blk0 + b, h)),
    ]
    args = [lam, q, cos_q, sin_q, subln, k, v]
    if latent:
        in_specs += [pl.BlockSpec((T, LANES), lambda b, h, i: (b, h)),
                     pl.BlockSpec((T, LANES), lambda b, h, i: (b, h))]
        args += [k, v]
    return pl.pallas_call(
        functools.partial(_attn_kernel, n_kblk=n_kblk, out_scale=out_scale),
        grid=(B, HA, n_q),
        in_specs=in_specs,
        out_specs=pl.BlockSpec((tq, LANES), lambda b, h, i: (b * n_q + i, h)),
        out_shape=jax.ShapeDtypeStruct((B * n_q * tq, D_A), BF16),
        compiler_params=_cparams(("arbitrary", "arbitrary", "arbitrary")),
        name="attn_lat" if latent else "attn_ctx",
    )(*args)


def _dft2_kernel(a_ref, z_ref, o_ref, acc_ref, *, nk, scale):
    k = pl.program_id(2)

    @pl.when(k == 0)
    def _():
        acc_ref[...] = jnp.zeros_like(acc_ref)

    acc_ref[...] += _dot(a_ref[...], z_ref[0])

    @pl.when(k == nk - 1)
    def _():
        o_ref[...] = (acc_ref[...] * scale).astype(o_ref.dtype)


def _dft2(a_mat, z, n_seq, t_len, row0):
    tm = min(1024, t_len)
    tk = min(2048, t_len)
    kb = t_len // tk
    nk = 2 * kb
    blk0 = row0 // tk
    scale = 1.0 / math.sqrt(t_len * F_GROUP_W)
    return pl.pallas_call(
        functools.partial(_dft2_kernel, nk=nk, scale=scale),
        grid=(n_seq, t_len // tm, nk),
        in_specs=[pl.BlockSpec((tm, tk), lambda b, i, k: (i, k)),
                  pl.BlockSpec((1, tk, D_F), lambda b, i, k: (k // kb, blk0 + b * kb + k % kb, 0))],
        out_specs=pl.BlockSpec((tm, D_F), lambda b, i, k: (b * (t_len // tm) + i, 0)),
        out_shape=jax.ShapeDtypeStruct((n_seq * t_len, D_F), BF16),
        scratch_shapes=[pltpu.VMEM((tm, D_F), F32)],
        compiler_params=_cparams(("arbitrary", "arbitrary", "arbitrary")),
        name="dft_positions",
    )(a_mat, z)


def _dft_tables(t_len):
    n_hi = t_len // LANES
    k = jnp.arange(t_len, dtype=jnp.int32)[:, None]
    unit = 2.0 * math.pi / t_len
    a_hi = ((k * (jnp.arange(n_hi, dtype=jnp.int32) * LANES)[None, :]) % t_len).astype(F32) * unit
    a_lo = ((k * jnp.arange(LANES, dtype=jnp.int32)[None, :]) % t_len).astype(F32) * unit

    def table_kernel(c1_ref, s1_ref, c2_ref, s2_ref, o_ref):
        c2, s2 = c2_ref[...], s2_ref[...]
        for hi in range(n_hi):
            c1, s1 = c1_ref[:, hi:hi + 1], s1_ref[:, hi:hi + 1]
            o_ref[:, hi * LANES:(hi + 1) * LANES] = (c1 * c2 - s1 * s2).astype(BF16)
            o_ref[:, t_len + hi * LANES:t_len + (hi + 1) * LANES] = (-(s1 * c2 + c1 * s2)).astype(BF16)

    row = lambda i: (i, 0)
    return pl.pallas_call(
        table_kernel,
        grid=(t_len // TM,),
        in_specs=[pl.BlockSpec((TM, n_hi), row), pl.BlockSpec((TM, n_hi), row),
                  pl.BlockSpec((TM, LANES), row), pl.BlockSpec((TM, LANES), row)],
        out_specs=pl.BlockSpec((TM, 2 * t_len), row),
        out_shape=jax.ShapeDtypeStruct((t_len, 2 * t_len), BF16),
        compiler_params=_cparams(("arbitrary",)),
        name="dft_table",
    )(jnp.cos(a_hi), jnp.sin(a_hi), jnp.cos(a_lo), jnp.sin(a_lo))


def _rwkv_prep_kernel(cs_ref, csp_ref, csn_ref, co_ref, cop_ref, con_ref, mucs_ref, muco_ref,
                      w0_ref, w2_ref, a0_ref, a2_ref, g2_ref, kk_ref, ka_ref, rk_ref, bd_ref,
                      lw_ref, kd_ref, bb_ref, v_ref, kn_ref, r_ref, bonus_ref, g_ref, *, st):
    i = pl.program_id(0)
    first, last = st.seq_pos(i)

    def tshift(x_ref, p_ref, n_ref, mu_ref):
        x = x_ref[...]
        prev_row = jnp.where(first, 0.0, p_ref[HALO - 1:HALO, :])
        next_row = jnp.where(last, 0.0, n_ref[0:1, :])
        prev, nxt = _shift_rows(x, prev_row, next_row)
        return x + mu_ref[0:1, :] * (prev - x) + mu_ref[1:2, :] * (nxt - x)

    cs = tshift(cs_ref, csp_ref, csn_ref, mucs_ref)
    co = tshift(co_ref, cop_ref, con_ref, muco_ref)
    k = cs[:, :D_C]
    v = cs[:, D_C:2 * D_C]
    wd = cs[:, 2 * D_C:2 * D_C + 2 * LORA_W]
    ad = cs[:, 2 * D_C + 2 * LORA_W:]
    r = co[:, :D_C]
    gd = co[:, D_C:]
    bd = bd_ref[...]

    zw = _dot3w(jnp.tanh(wd), w2_ref)
    za = _dot3w(ad, a2_ref)
    kx = k * kk_ref[...]
    ss = _dot_exact_rhs(kx * kx, bd)
    kn = kx / jnp.maximum(jnp.sqrt(ss), 1e-12)
    kd_sum = jnp.zeros_like(k)
    for d in range(2):
        sl = slice(d * D_C, (d + 1) * D_C)
        z = -(w0_ref[d:d + 1, :] + zw[:, sl])
        softplus = jnp.maximum(z, 0.0) + jnp.log(1.0 + jnp.exp(-jnp.abs(z)))
        lw_ref[d] = -jnp.exp(-softplus - 0.5)
        a = _sigmoid(a0_ref[d:d + 1, :] + za[:, sl])
        kd = k * (1.0 + (a - 1.0) * ka_ref[...])
        kd_ref[d] = kd
        bb_ref[d] = kn * a
        kd_sum = kd_sum + kd
    v_ref[...] = v
    kn_ref[...] = kn
    r_ref[...] = r
    bonus_ref[...] = _dot_exact_rhs(r * kd_sum * rk_ref[...], bd) * v
    g_ref[...] = _dot3w(_sigmoid(gd), g2_ref)


def _rwkv_prep(cs, co, lp, st):
    n = st.n_rows
    row = lambda i: (i, 0)
    full = lambda shape: pl.BlockSpec(shape, lambda i: (0,) * len(shape))
    csp, csn = _halo_specs(D_CS, 0, n)
    cop, con = _halo_specs(D_CO, 0, n)
    dir_out = pl.BlockSpec((2, TM, D_C), lambda i: (0, i, 0))
    tok_out = pl.BlockSpec((TM, D_C), row)
    return pl.pallas_call(
        functools.partial(_rwkv_prep_kernel, st=st),
        grid=(st.tiles,),
        in_specs=[pl.BlockSpec((TM, D_CS), row), csp, csn, pl.BlockSpec((TM, D_CO), row), cop, con,
                  full((2, D_CS)), full((2, D_CO)), full((2, D_C)), full((2, 2 * LORA_W, 2 * D_C)),
                  full((2, D_C)), full((2, 2 * LORA_A, 2 * D_C)), full((2, LORA_G, D_C)),
                  full((1, D_C)), full((1, D_C)), full((1, D_C)), full((D_C, D_C))],
        out_specs=[dir_out, dir_out, dir_out, tok_out, tok_out, tok_out, tok_out, tok_out],
        out_shape=[jax.ShapeDtypeStruct((2, n, D_C), F32)] * 3 + [jax.ShapeDtypeStruct((n, D_C), F32)] * 5,
        compiler_params=_cparams(("arbitrary",)),
        name="rwkv_prep",
    )(cs, cs, cs, co, co, co, lp["mu_cs"], lp["mu_co"], lp["w0"], lp["w2bd"], lp["a0"], lp["a2bd"],
      lp["g2"], lp["k_k"], lp["k_a"], lp["r_k"], lp["head_ones"])


def _rwkv_chunk_kernel(lw_ref, kd_ref, bb_ref, v_ref, kn_ref, r_ref, qh_ref, y0_ref, g_ref, j_ref):
    d = pl.program_id(0)
    L = CHUNK
    ti = lax.broadcasted_iota(jnp.int32, (L, L), 0)
    si = lax.broadcasted_iota(jnp.int32, (L, L), 1)
    sgn = jnp.where(d == 0, 1, -1)
    incl = jnp.where((ti - si) * sgn >= 0, 1.0, 0.0).astype(BF16)
    ri = lax.broadcasted_iota(jnp.int32, (PAIR, PAIR), 0)
    ci = lax.broadcasted_iota(jnp.int32, (PAIR, PAIR), 1)
    same_head = (ri // HS_C) == (ci // HS_C)
    lag = jnp.where(same_head, (ri % L - ci % L) * sgn, -1)
    m_strict = lag > 0
    m_incl = lag >= 0
    eye = (ri == ci).astype(F32)
    lane = lax.broadcasted_iota(jnp.int32, (L, D_C), 1)
    head0 = (lane % PAIR) < HS_C

    def stack(x):
        return jnp.concatenate([jnp.where(head0, x, 0.0), jnp.where(head0, 0.0, x)], axis=0)

    def fold(x):
        return x[:L] + x[L:]

    pairs = range(N_PAIR)
    sls = [slice(p * PAIR, (p + 1) * PAIR) for p in pairs]
    H = PAIR
    units = [(c, p) for c in range(CPS) for p in pairs]

    def wide(c):
        rs = slice(c * L, (c + 1) * L)
        lw, kd, bb = lw_ref[0, rs, :], kd_ref[0, rs, :], bb_ref[0, rs, :]
        v, kn, r = v_ref[rs, :], kn_ref[rs, :], r_ref[rs, :]
        cum = _dot_exact_rhs_lhs(incl, lw)
        tot = jnp.sum(lw, axis=0, keepdims=True)
        e_inv = jnp.exp(-cum)
        e_end = jnp.exp(tot - cum)
        xr_f = stack(r * jnp.exp(cum))
        a_t = -bb * e_inv
        k_t = kd * e_inv
        return dict(
            e_tot=jnp.exp(tot), xr_f=xr_f,
            rows=jnp.concatenate([stack(kn * jnp.exp(cum - lw)), xr_f], axis=0).astype(BF16),
            cols=jnp.concatenate([a_t, a_t, k_t, k_t], axis=0).astype(BF16),
            vs=stack(v).astype(BF16),
            ends=jnp.concatenate([-bb * e_end, kd * e_end], axis=0).astype(BF16),
            v_b=v.astype(BF16))

    ch = [wide(c) for c in range(CPS)]
    big = [_dot_nt(ch[c]["rows"][:, sls[p]], ch[c]["cols"][:, sls[p]]) for c, p in units]
    n_mat = [jnp.where(m_strict, x[:H, :H], 0.0) for x in big]
    m_bk = [jnp.where(m_strict, x[:H, H:], 0.0).astype(BF16) for x in big]
    m_ra = [jnp.where(m_incl, x[H:, :H], 0.0).astype(BF16) for x in big]
    m_rk = [jnp.where(m_incl, x[H:, H:], 0.0).astype(BF16) for x in big]
    n_u = range(len(units))
    u1 = [_dot(m_bk[u], ch[c]["vs"][:, sls[p]]) for u, (c, p) in enumerate(units)]
    t_mat = [eye + x for x in n_mat]
    n_b = [x.astype(BF16) for x in n_mat]
    n_pow = [_dot(x, x).astype(BF16) for x in n_b]
    span = 2
    while span < L // 2:
        res = [_dot(jnp.concatenate([n_pow[u], t_mat[u].astype(BF16)], axis=0), n_pow[u]) for u in n_u]
        n_pow = [x[:H].astype(BF16) for x in res]
        t_mat = [t_mat[u] + res[u][H:] for u in n_u]
        span *= 2
    t_mat = [t_mat[u] + _dot(t_mat[u].astype(BF16), n_pow[u]) for u in n_u]
    wu = [_dot(t_mat[u].astype(BF16), jnp.concatenate([ch[c]["rows"][:H, sls[p]], u1[u].astype(BF16)], axis=1))
          for u, (c, p) in enumerate(units)]
    qy = [_dot(m_ra[u], wu[u].astype(BF16)) for u in n_u]
    y0x = [_dot(m_rk[u], ch[c]["vs"][:, sls[p]]) for u, (c, p) in enumerate(units)]
    zeros = jnp.zeros((L, H), BF16)
    gj = [_dot_tn(ch[c]["ends"][:, sls[p]],
                  jnp.concatenate([fold(wu[u]).astype(BF16),
                                   jnp.concatenate([zeros, ch[c]["v_b"][:, sls[p]]], axis=1)], axis=0))
          for u, (c, p) in enumerate(units)]
    for u, (c, p) in enumerate(units):
        pos = jnp.where(d == 0, c, CPS - 1 - c)
        qh_ref[0, pos, 0, :, sls[p]] = fold(ch[c]["xr_f"][:, sls[p]] + qy[u][:, :H])
        y0_ref[0, pos, 0, :, sls[p]] = fold(qy[u][:, H:] + y0x[u])
        g_ref[0, pos, 0, p] = jnp.where(same_head, gj[u][:, :H], 0.0) + eye * ch[c]["e_tot"][:, sls[p]]
        j_ref[0, pos, 0, p] = jnp.where(same_head, gj[u][:, H:], 0.0)


def _dot_exact_rhs_lhs(mask_bf16, x):
    x1 = x.astype(BF16)
    r1 = x - x1.astype(F32)
    x2 = r1.astype(BF16)
    x3 = (r1 - x2.astype(F32)).astype(BF16)
    return _dot(mask_bf16, x1) + (_dot(mask_bf16, x2) + _dot(mask_bf16, x3))


def _scan_geometry(st):
    nc_ctx = st.Lc // CHUNK
    nc_lat = st.T // CHUNK
    ns = nc_ctx + nc_lat
    ctx_base = st.n_lat_rows // CHUNK

    def row_block(d, b, s):
        in_ctx = s < nc_ctx
        j_ctx = jnp.where(d == 0, s, nc_ctx - 1 - s)
        j_lat = jnp.where(d == 0, s - nc_ctx, nc_lat - 1 - (s - nc_ctx))
        return jnp.where(in_ctx, ctx_base + b * nc_ctx + j_ctx, b * nc_lat + j_lat)

    return nc_ctx, nc_lat, ns, row_block


def _rwkv_chunks(lw, kd, bb, v, kn, r, st):
    B = st.B
    nc_ctx, nc_lat, ns, row_block = _scan_geometry(st)
    assert nc_ctx % CPS == 0 and nc_lat % CPS == 0
    rows = CPS * CHUNK
    blk = lambda d, b, s: row_block(d, b, s * CPS) // CPS
    dir_in = pl.BlockSpec((1, rows, D_C), lambda d, b, s: (d, blk(d, b, s), 0))
    tok_in = pl.BlockSpec((rows, D_C), lambda d, b, s: (blk(d, b, s), 0))
    row_out = pl.BlockSpec((1, CPS, 1, CHUNK, D_C), lambda d, b, s: (d, s, b, 0, 0))
    mat_out = pl.BlockSpec((1, CPS, 1, N_PAIR, PAIR, PAIR), lambda d, b, s: (d, s, b, 0, 0, 0))
    return pl.pallas_call(
        _rwkv_chunk_kernel,
        grid=(2, B, ns // CPS),
        in_specs=[dir_in, dir_in, dir_in, tok_in, tok_in, tok_in],
        out_specs=[row_out, row_out, mat_out, mat_out],
        out_shape=[jax.ShapeDtypeStruct((2, ns, B, CHUNK, D_C), F32)] * 2
        + [jax.ShapeDtypeStruct((2, ns, B, N_PAIR, PAIR, PAIR), F32)] * 2,
        compiler_params=_cparams(("arbitrary", "arbitrary", "arbitrary")),
        name="rwkv_chunks",
    )(lw, kd, bb, v, kn, r)


def _rwkv_carry_kernel(qh_ref, y0_ref, g_ref, j_ref, y_ref, h_ref):
    s = pl.program_id(0)

    @pl.when(s == 0)
    def _():
        h_ref[...] = jnp.zeros_like(h_ref)

    nb = qh_ref.shape[2]
    for d in range(2):
        for b in range(nb):
            for p in range(N_PAIR):
                sl = slice(p * PAIR, (p + 1) * PAIR)
                hh, hl = _split2(h_ref[d, b, p])
                gh, gl = _split2(g_ref[d, 0, b, p])
                top = _dot(jnp.concatenate([qh_ref[d, 0, b, :, sl].astype(BF16), gh], axis=0), hh)
                y_ref[d, 0, b, :, sl] = top[:CHUNK] + y0_ref[d, 0, b, :, sl]
                h_ref[d, b, p] = top[CHUNK:] + (_dot(gh, hl) + _dot(gl, hh)) + j_ref[d, 0, b, p]


def _rwkv_carry(qh, y0, g, j, st):
    B = st.B
    ns = qh.shape[1]
    rows = pl.BlockSpec((2, 1, B, CHUNK, D_C), lambda s: (0, s, 0, 0, 0))
    mats = pl.BlockSpec((2, 1, B, N_PAIR, PAIR, PAIR), lambda s: (0, s, 0, 0, 0, 0))
    return pl.pallas_call(
        _rwkv_carry_kernel,
        grid=(ns,),
        in_specs=[rows, rows, mats, mats],
        out_specs=rows,
        out_shape=jax.ShapeDtypeStruct((2, ns, B, CHUNK, D_C), F32),
        scratch_shapes=[pltpu.VMEM((2, B, N_PAIR, PAIR, PAIR), F32)],
        compiler_params=_cparams(("arbitrary",)),
        name="rwkv_carry",
    )(qh, y0, g, j)


def _mix_out_kernel(oa_ref, of_ref, yf_ref, yb_ref, bonus_ref, rg_ref, gate_ref, x_ref, mod_ref, gnw_ref, gnb_ref,
                    bd_ref, wpa_ref, wpf_ref, wpc_ref, wo_ref, lng_ref, lnb_ref, x1_ref, h2_ref, *, alpha):
    n = TM // CHUNK
    y = jnp.concatenate([yf_ref[0, c, 0] + yb_ref[0, n - 1 - c, 0] for c in range(n)], axis=0)
    bd = bd_ref[...]
    mu = _dot_exact_rhs(y, bd) * (1.0 / HS_C)
    yc = y - mu
    var = _dot_exact_rhs(yc * yc, bd) * (1.0 / HS_C)
    yn = yc * lax.rsqrt(var + GN_EPS) * gnw_ref[...] + gnb_ref[...]
    o_c = ((yn + bonus_ref[...]) * rg_ref[...]).astype(BF16)
    g0 = gate_ref[:, 0:D_MODEL].astype(F32)
    g1 = gate_ref[:, D_MODEL:2 * D_MODEL].astype(F32)
    g2 = gate_ref[:, 2 * D_MODEL:].astype(F32)
    y = g0 * _dot(oa_ref[...], wpa_ref[...]) + g1 * _dot(of_ref[...], wpf_ref[...]) + g2 * _dot(o_c, wpc_ref[...])
    mix = _dot(y.astype(BF16), wo_ref[...])
    z = alpha * x_ref[...] + mod_ref[0, 2:3, :] * mix
    x1 = _ln_plain(z) * lng_ref[...] + lnb_ref[...]
    x1_ref[...] = x1
    h2_ref[...] = (_ln_plain(x1) * (1.0 + mod_ref[0, 4:5, :]) + mod_ref[0, 3:4, :]).astype(BF16)


def _mix_out(o_a, o_f, y, bonus, rg, gate, x_all, mod, lp, st, n_tiles, alpha):
    nc_ctx, nc_lat, _, _ = _scan_geometry(st)
    n = TM // CHUNK
    assert nc_ctx % n == 0

    def seq_of(i):
        lat = i < st.lat_tiles
        b = jnp.where(lat, i // st.tpb, (i - st.lat_tiles) // st.tpc)
        j = jnp.where(lat, i % st.tpb, (i - st.lat_tiles) % st.tpc)
        return lat, b, j

    def fwd(i):
        lat, b, j = seq_of(i)
        return (0, jnp.where(lat, nc_ctx // n + j, j), b, 0, 0)

    def bwd(i):
        lat, b, j = seq_of(i)
        return (1, jnp.where(lat, nc_ctx // n + (st.tpb - 1 - j), st.tpc - 1 - j), b, 0, 0)

    row = lambda i: (i, 0)
    full = lambda shape: pl.BlockSpec(shape, lambda i: (0,) * len(shape))
    return pl.pallas_call(
        functools.partial(_mix_out_kernel, alpha=alpha),
        grid=(n_tiles,),
        in_specs=[pl.BlockSpec((TM, D_A), row), pl.BlockSpec((TM, D_F), row),
                  pl.BlockSpec((1, n, 1, CHUNK, D_C), fwd), pl.BlockSpec((1, n, 1, CHUNK, D_C), bwd),
                  pl.BlockSpec((TM, D_C), row), pl.BlockSpec((TM, D_C), row),
                  pl.BlockSpec((TM, N_BRANCH * D_MODEL), row), pl.BlockSpec((TM, D_MODEL), row),
                  pl.BlockSpec((1, 6, D_MODEL), lambda i: (st.mod_index(i), 0, 0)),
                  full((1, D_C)), full((1, D_C)), full((D_C, D_C)),
                  full((D_A, D_MODEL)), full((D_F, D_MODEL)), full((D_C, D_MODEL)), full((D_MODEL, D_MODEL)),
                  full((1, D_MODEL)), full((1, D_MODEL))],
        out_specs=[pl.BlockSpec((TM, D_MODEL), row), pl.BlockSpec((TM, D_MODEL), row)],
        out_shape=[jax.ShapeDtypeStruct((n_tiles * TM, D_MODEL), F32),
                   jax.ShapeDtypeStruct((n_tiles * TM, D_MODEL), BF16)],
        compiler_params=_cparams(("arbitrary",)),
        name="mix_out",
    )(o_a, o_f, y, y, bonus, rg, gate, x_all, mod, lp["ln_w"], lp["ln_b"], lp["head_ones"],
      lp["w_pa"], lp["w_pf"], lp["w_pc"], lp["w_o"], lp["ln1_g"], lp["ln1_b"])


def _ffn_down_kernel(u_ref, g_ref, gp_ref, gn_ref, x_ref, mod_ref, cw_ref, cb_ref, wd_ref, lng_ref, lnb_ref,
                     o_ref, *, st, alpha):
    i = pl.program_id(0)
    first, last = st.seq_pos(i)
    g = g_ref[...].astype(F32)
    prev_row = jnp.where(first, 0.0, gp_ref[HALO - 1:HALO, :].astype(F32))
    next_row = jnp.where(last, 0.0, gn_ref[0:1, :].astype(F32))
    prev, nxt = _shift_rows(g, prev_row, next_row)
    cw = cw_ref[...] * 0.5
    gh = cw[0:1, :] * prev + cw[1:2, :] * g + cw[2:3, :] * nxt + cb_ref[...] * 0.5
    act = gh * (1.0 + lax.erf(gh * (2.0 ** 0.5)))
    a = (act * u_ref[...].astype(F32)).astype(BF16)
    z = alpha * x_ref[...] + mod_ref[0, 5:6, :] * _dot(a, wd_ref[...])
    o_ref[...] = _ln_plain(z) * lng_ref[...] + lnb_ref[...]


def _ffn_down(ug, x1, mod, lp, st, n_tiles, alpha):
    row = lambda i: (i, 0)
    full = lambda shape: pl.BlockSpec(shape, lambda i: (0,) * len(shape))
    gp, gn = _halo_specs(D_FF, 1, ug.shape[0])
    return pl.pallas_call(
        functools.partial(_ffn_down_kernel, st=st, alpha=alpha),
        grid=(n_tiles,),
        in_specs=[pl.BlockSpec((TM, D_FF), lambda i: (i, 0)), pl.BlockSpec((TM, D_FF), lambda i: (i, 1)), gp, gn,
                  pl.BlockSpec((TM, D_MODEL), row),
                  pl.BlockSpec((1, 6, D_MODEL), lambda i: (st.mod_index(i), 0, 0)),
                  full((3, D_FF)), full((1, D_FF)), full((D_FF, D_MODEL)), full((1, D_MODEL)), full((1, D_MODEL))],
        out_specs=pl.BlockSpec((TM, D_MODEL), row),
        out_shape=jax.ShapeDtypeStruct((n_tiles * TM, D_MODEL), F32),
        compiler_params=_cparams(("arbitrary",)),
        name="ffn_down",
    )(ug, ug, ug, ug, x1, mod, lp["ffn_conv"], lp["ffn_conv_b"], lp["ffn_down"], lp["ln2_g"], lp["ln2_b"])


def _rope_tables(st):
    T = st.T
    pos = jnp.arange(T)
    rowp = (pos // GRID_W).astype(F32)
    colp = (pos % GRID_W).astype(F32)
    n_freq = DH_A // 4
    inv = ROPE_BASE ** (-jnp.arange(n_freq, dtype=F32) / n_freq)
    ar, ac = rowp[:, None] * inv, colp[:, None] * inv
    cos64 = jnp.concatenate([jnp.cos(ar), jnp.cos(ar), jnp.cos(ac), jnp.cos(ac)], axis=1)
    sin64 = jnp.concatenate([-jnp.sin(ar), jnp.sin(ar), -jnp.sin(ac), jnp.sin(ac)], axis=1)
    cos_q = jnp.concatenate([cos64, cos64], axis=1)
    sin_q = jnp.concatenate([sin64, sin64], axis=1)
    return cos_q, sin_q


def _block_diag2(m):
    z = jnp.zeros_like(m[0])
    return jnp.concatenate([jnp.concatenate([m[0], z], axis=1), jnp.concatenate([z, m[1]], axis=1)], axis=0)


def _hi_lo(w):
    return jnp.stack(_split2(w))


def kernel(x, c, ctx, c_ctx, ada_w, ada_b, w_in, lam_qk, subln_g, tshift_mu, rw_w0, rw_w2, rw_a0, rw_a2, rw_g2, rw_kk, rw_ka, rw_rk, rw_lnw, rw_lnb, w_pa, w_pf, w_pc, w_o, ln1_g, ln1_b, ffn_up, ffn_conv, ffn_conv_b, ffn_down, ln2_g, ln2_b):
    B, T, D = x.shape
    Lc = ctx.shape[1]
    depth = w_in.shape[0]
    st = _Stream(B, T, Lc)
    alpha = (2.0 * depth) ** 0.25

    cos_q, sin_q = _rope_tables(st)
    dft_lat = _dft_tables(T)
    dft_ctx = _dft_tables(Lc)
    kf = jnp.arange(F_GROUP_W, dtype=jnp.int32)
    ang = ((kf[:, None] * kf[None, :]) % F_GROUP_W).astype(F32) * (2.0 * math.pi / F_GROUP_W)
    dft_ch = jnp.concatenate([jnp.cos(ang), jnp.sin(ang)], axis=1).astype(BF16)
    hi = jnp.arange(D_C) // HS_C
    head_ones = (hi[:, None] == hi[None, :]).astype(BF16)
    c_all = jnp.concatenate([c, c_ctx[None], jnp.zeros((8 - B - 1, D), F32)], axis=0)

    x_all = jnp.concatenate([x.reshape(B * T, D), ctx.reshape(B * Lc, D)], axis=0)
    for i in range(depth):
        last = i == depth - 1
        lam_init = 0.8 - 0.6 * math.exp(-0.3 * i)
        lq = lam_qk[i].astype(F32)
        lam = (jnp.exp(jnp.sum(lq[0] * lq[1])) - jnp.exp(jnp.sum(lq[2] * lq[3])) + lam_init).reshape(1, 1)
        wi = w_in[i].astype(BF16)
        lp = {
            "mu_cs": tshift_mu[i][:, :D_CS], "mu_co": tshift_mu[i][:, D_CS:],
            "w0": rw_w0[i], "w2bd": _hi_lo(_block_diag2(rw_w2[i])), "a0": rw_a0[i],
            "a2bd": _hi_lo(_block_diag2(rw_a2[i])),
            "g2": _hi_lo(rw_g2[i]), "k_k": rw_kk[i].reshape(1, D_C), "k_a": rw_ka[i].reshape(1, D_C),
            "r_k": rw_rk[i].reshape(1, D_C), "ln_w": rw_lnw[i].reshape(1, D_C), "ln_b": rw_lnb[i].reshape(1, D_C),
            "head_ones": head_ones,
            "w_pa": w_pa[i].astype(BF16), "w_pf": w_pf[i].astype(BF16), "w_pc": w_pc[i].astype(BF16),
            "w_o": w_o[i].astype(BF16), "ln1_g": ln1_g[i].reshape(1, D), "ln1_b": ln1_b[i].reshape(1, D),
            "ffn_conv": ffn_conv[i], "ffn_conv_b": ffn_conv_b[i].reshape(1, D_FF),
            "ffn_down": ffn_down[i].astype(BF16), "ln2_g": ln2_g[i].reshape(1, D), "ln2_b": ln2_b[i].reshape(1, D),
        }
        n_out = st.lat_tiles if last else st.tiles

        mod = _adaln(c_all, ada_w[i], ada_b[i]).reshape(8, 6, D)
        q, k, v, z = _proj(x_all, wi[:, :OFF_CS], ((D_A, 512, "plain", F32), (D_A, 512, "rope", BF16),
                                                    (D_A, 512, "plain", BF16), (D_F, 512, "dft", BF16)),
                           (cos_q, sin_q, dft_ch), mod, st)
        cs, co, gate = _proj(x_all, wi[:, OFF_CS:], ((D_CS, D_CS // 2, "plain", F32), (D_CO, D_CO, "plain", F32),
                                                       (N_BRANCH * D_MODEL, 512, "sigmoid", BF16)), (), mod, st)

        subln = subln_g[i].reshape(1, DV_A)
        o_a = _attention(lam, q, k, v, cos_q, sin_q, subln, st, 1.0 - lam_init, True)
        o_f = _dft2(dft_lat, z, B, T, 0)
        if not last:
            o_a = jnp.concatenate([o_a, _attention(lam, q, k, v, cos_q, sin_q, subln, st, 1.0 - lam_init, False)], axis=0)
            o_f = jnp.concatenate([o_f, _dft2(dft_ctx, z, B, Lc, st.n_lat_rows)], axis=0)

        lw, kd, bb, rv, kn, rr, bonus, rg = _rwkv_prep(cs, co, lp, st)
        qh, y0, gm, jm = _rwkv_chunks(lw, kd, bb, rv, kn, rr, st)
        y = _rwkv_carry(qh, y0, gm, jm, st)
        x1, h2 = _mix_out(o_a, o_f, y, bonus, rg, gate, x_all, mod, lp, st, n_out, alpha)
        (ug,) = _proj(h2, ffn_up[i].astype(BF16), ((2 * D_FF, 512, "plain", BF16),))
        x_all = _ffn_down(ug, x1, mod, lp, st, n_out, alpha)
    return x_all[:B * T].reshape(B, T, D)
```

```python
import functools
import math

import jax
import jax.numpy as jnp
from jax import lax
from jax.experimental import pallas as pl
from jax.experimental.pallas import tpu as pltpu

F32 = jnp.float32
BF16 = jnp.bfloat16

D_MODEL = 1024
DEPTH = 2
GRID_W = 64
HA = 8
DH_A = 64
DV_A = 2 * DH_A
D_A = HA * DV_A
ROPE_BASE = 10000.0
F_GROUPS = 4
F_GROUP_W = 128
D_F = F_GROUPS * F_GROUP_W
HC = 8
HS_C = 64
D_C = HC * HS_C
LORA_W = 64
LORA_A = 64
LORA_G = 128
D_CS = 2 * D_C + 2 * LORA_W + 2 * LORA_A
D_CO = D_C + LORA_G
N_BRANCH = 3
D_FF = 2816
LN_EPS = 1e-5
GN_EPS = HS_C * 1e-5
HEAD_NORM_EPS = 1e-5

OFF_Q = 0
OFF_K = OFF_Q + D_A
OFF_V = OFF_K + D_A
OFF_F = OFF_V + D_A
OFF_CS = OFF_F + D_F
OFF_CO = OFF_CS + D_CS
OFF_GATE = OFF_CO + D_CO
D_IN = OFF_GATE + N_BRANCH * D_MODEL

LANES = 128
HALO = 16
TM = 256
TMP = 512
CHUNK = 64
PAIR = 2 * HS_C
CPS = 4
N_PAIR = D_C // PAIR
VMEM_LIMIT = 56 * 1024 * 1024


def _cparams(sem):
    return pltpu.CompilerParams(dimension_semantics=sem, vmem_limit_bytes=VMEM_LIMIT)


def _split2(x):
    hi = x.astype(BF16)
    lo = (x - hi.astype(F32)).astype(BF16)
    return hi, lo


def _dot(a, b, dims=(((1,), (0,)), ((), ()))):
    return lax.dot_general(a, b, dims, preferred_element_type=F32)


def _dot_nt(a, b):
    return lax.dot_general(a, b, (((1,), (1,)), ((), ())), preferred_element_type=F32)


def _dot_tn(a, b):
    return lax.dot_general(a, b, (((0,), (0,)), ((), ())), preferred_element_type=F32)


def _dotb(a, b):
    return _dot(a.astype(BF16), b.astype(BF16))


def _dot3(a, b):
    ah, al = _split2(a)
    bh, bl = _split2(b)
    return _dot(ah, bh) + (_dot(ah, bl) + _dot(al, bh))


def _dot3w(a, w_ref):
    ah, al = _split2(a)
    return _dot(ah, w_ref[0]) + (_dot(ah, w_ref[1]) + _dot(al, w_ref[0]))


def _dot_exact_rhs(a, b_bf16):
    a1, a2 = _split2(a)
    return _dot(a1, b_bf16) + _dot(a2, b_bf16)


def _ln_plain(x):
    mu = jnp.mean(x, axis=-1, keepdims=True)
    xc = x - mu
    var = jnp.mean(xc * xc, axis=-1, keepdims=True)
    return xc * lax.rsqrt(var + LN_EPS)


def _sigmoid(x):
    return 1.0 / (1.0 + jnp.exp(-x))


def _swap16(x):
    n = x.shape[-1]
    lane = lax.broadcasted_iota(jnp.int32, x.shape, x.ndim - 1)
    up = pltpu.roll(x, n - 16, axis=x.ndim - 1)
    dn = pltpu.roll(x, 16, axis=x.ndim - 1)
    return jnp.where((lane % 32) < 16, up, dn)


def _shift_rows(x, prev_row, next_row):
    n = x.shape[0]
    row = lax.broadcasted_iota(jnp.int32, x.shape, 0)
    prev = jnp.where(row == 0, prev_row, pltpu.roll(x, 1, axis=0))
    nxt = jnp.where(row == n - 1, next_row, pltpu.roll(x, n - 1, axis=0))
    return prev, nxt


class _Stream:
    def __init__(self, B, T, Lc):
        assert T % TM == 0 and Lc % TM == 0 and (B * T) % Lc == 0
        self.B, self.T, self.Lc = B, T, Lc
        self.n_lat_rows = B * T
        self.n_rows = B * T + B * Lc
        self.lat_tiles = B * T // TM
        self.tiles = self.n_rows // TM
        self.tpb = T // TM
        self.tpc = Lc // TM

    def mod_index(self, i):
        return jnp.where(i < self.lat_tiles, i // self.tpb, self.B)

    def seq_pos(self, i):
        j = jnp.where(i < self.lat_tiles, i % self.tpb, (i - self.lat_tiles) % self.tpc)
        n = jnp.where(i < self.lat_tiles, self.tpb, self.tpc)
        return j == 0, j == n - 1


def _pair(a):
    return a if isinstance(a, tuple) else (a,)


def _stream_specs(parts, tile, width, n_lat_tiles):
    if len(parts) == 1:
        return [pl.BlockSpec((tile, width), lambda i: (i, 0))]
    return [pl.BlockSpec((tile, width), lambda i: (jnp.minimum(i, n_lat_tiles - 1), 0)),
            pl.BlockSpec((tile, width), lambda i: (jnp.maximum(i - n_lat_tiles, 0), 0))]


def _stream_tile(refs, n_lat_tiles):
    if len(refs) == 1:
        return refs[0][...]
    return jnp.where(pl.program_id(0) < n_lat_tiles, refs[0][...], refs[1][...])


def _halo_specs(width, col_block, n_rows):
    per = TM // HALO
    last = n_rows // HALO - 1
    prev = pl.BlockSpec((HALO, width), lambda i: (jnp.maximum(i * per - 1, 0), col_block))
    nxt = pl.BlockSpec((HALO, width), lambda i: (jnp.minimum((i + 1) * per, last), col_block))
    return prev, nxt


def _adaln_kernel(c_ref, w_ref, b_ref, o_ref):
    c = c_ref[...]
    s = c * _sigmoid(c)
    o_ref[...] = _dot3(s, w_ref[...]) + b_ref[...]


def _adaln(c_all, w, b):
    n = w.shape[1]
    tn = 1536
    return pl.pallas_call(
        _adaln_kernel,
        grid=(n // tn,),
        in_specs=[pl.BlockSpec((8, D_MODEL), lambda j: (0, 0)),
                  pl.BlockSpec((D_MODEL, tn), lambda j: (0, j)),
                  pl.BlockSpec((1, tn), lambda j: (0, j))],
        out_specs=pl.BlockSpec((8, tn), lambda j: (0, j)),
        out_shape=jax.ShapeDtypeStruct((8, n), F32),
        compiler_params=_cparams(("arbitrary",)),
        name="adaln",
    )(c_all, w, b.reshape(1, n))


def _cast_kernel(w_ref, *o_refs):
    off = 0
    for o_ref in o_refs:
        width = o_ref.shape[-1]
        o_ref[...] = w_ref[:, :, off:off + width].astype(o_ref.dtype)
        off += width


def _cast_bf16(w, splits=None):
    n_l, rows, cols = w.shape
    splits = splits or (cols,)
    assert sum(splits) == cols and rows % TM == 0
    return pl.pallas_call(
        _cast_kernel,
        grid=(n_l, rows // TM),
        in_specs=[pl.BlockSpec((1, TM, cols), lambda l, r: (l, r, 0))],
        out_specs=[pl.BlockSpec((1, TM, c), lambda l, r: (l, r, 0)) for c in splits],
        out_shape=[jax.ShapeDtypeStruct((n_l, rows, c), BF16) for c in splits],
        compiler_params=_cparams(("arbitrary", "arbitrary")),
        name="cast_weights",
    )(w)


def _proj_kernel(*refs, n_src, groups, has_mod, n_tables, lat_tiles):
    a = _stream_tile(refs[:n_src], lat_tiles)
    w_ref, rest = refs[n_src], refs[n_src + 1:]
    if has_mod:
        mod_ref, rest = rest[0], rest[1:]
        a = (_ln_plain(a) * (1.0 + mod_ref[0, 1:2, :]) + mod_ref[0, 0:1, :]).astype(BF16)
    tables, outs = rest[:n_tables], rest[n_tables:]
    off = 0
    for (width, chunk, epi), o_ref in zip(groups, outs):
        for c0 in range(0, width, chunk):
            acc = _dot(a, w_ref[:, off + c0:off + c0 + chunk])
            if epi == "rope":
                reps = chunk // LANES
                lat = pl.program_id(0) < lat_tiles
                cos = jnp.where(lat, jnp.concatenate([tables[0][...]] * reps, axis=1), 1.0)
                sin = jnp.where(lat, jnp.concatenate([tables[1][...]] * reps, axis=1), 0.0)
                acc = acc * cos + _swap16(acc) * sin
            elif epi == "sigmoid":
                acc = _sigmoid(acc)
            if epi == "dft":
                fb = acc.astype(BF16)
                for g in range(chunk // F_GROUP_W):
                    sl = slice(c0 + g * F_GROUP_W, c0 + (g + 1) * F_GROUP_W)
                    r = _dot(fb[:, g * F_GROUP_W:(g + 1) * F_GROUP_W], tables[2][...])
                    o_ref[0, :, sl] = r[:, :F_GROUP_W].astype(o_ref.dtype)
                    o_ref[1, :, sl] = r[:, F_GROUP_W:].astype(o_ref.dtype)
            else:
                o_ref[:, c0:c0 + chunk] = acc.astype(o_ref.dtype)
        off += width


def _proj(h, w, layer, groups, tables=(), mod=None, st=None):
    srcs = _pair(h)
    m, k = sum(a.shape[0] for a in srcs), srcs[0].shape[1]
    n = w.shape[2]
    assert all(a.shape[0] % TMP == 0 for a in srcs) and sum(g[0] for g in groups) == n
    assert all(g[0] % g[1] == 0 for g in groups)
    row = lambda i: (i, 0)
    lat_tiles, per_seq = (st.n_lat_rows // TMP, st.T // TMP) if st is not None else (0, 1)
    specs = _stream_specs(srcs, TMP, k, lat_tiles) + [pl.BlockSpec((None, k, n), lambda i: (layer, 0, 0))]
    args = list(srcs) + [w]
    if mod is not None:
        specs.append(pl.BlockSpec((1, 6, D_MODEL), lambda i: (jnp.where(i < lat_tiles, i // per_seq, st.B), 0, 0)))
        args.append(mod)
    for t in tables:
        if st is not None and t.shape == (st.T, LANES):
            specs.append(pl.BlockSpec((TMP, LANES), lambda i: (jnp.where(i < lat_tiles, i % per_seq, 0), 0)))
        else:
            specs.append(pl.BlockSpec(t.shape, lambda i: (0, 0)))
    out_specs = [pl.BlockSpec((2, TMP, g[0]), lambda i: (0, i, 0)) if g[2] == "dft" else pl.BlockSpec((TMP, g[0]), row)
                 for g in groups]
    out_shape = [jax.ShapeDtypeStruct((2, m, g[0]) if g[2] == "dft" else (m, g[0]), g[3]) for g in groups]
    return pl.pallas_call(
        functools.partial(_proj_kernel, n_src=len(srcs), groups=tuple(g[:3] for g in groups), has_mod=mod is not None,
                          n_tables=len(tables), lat_tiles=lat_tiles),
        grid=(m // TMP,),
        in_specs=specs,
        out_specs=out_specs,
        out_shape=out_shape,
        compiler_params=_cparams(("arbitrary",)),
        name="proj_" + "_".join(g[2] for g in groups),
    )(*args, *tables)


TQ = 512
TK = 512
AHEAD = 1


def _attn_kernel(lam_ref, q_ref, cos_ref, sin_ref, g_ref, kc_ref, vc_ref, *rest, n_kblk, out_scale):
    if n_kblk:
        kl_ref, vl_ref, o_ref = rest
    else:
        (o_ref,) = rest
    tq = q_ref.shape[0]
    lane = lax.broadcasted_iota(jnp.int32, (tq, LANES), 1)
    first = lane < DH_A
    q = q_ref[...] * (DH_A ** -0.5 * math.log2(math.e))

    def stack(x):
        return jnp.concatenate([jnp.where(first, x, 0.0), jnp.where(first, 0.0, x)], axis=0).astype(BF16)

    def update(s, v, m, acc):
        m_new = jnp.maximum(m, jnp.max(s, axis=-1, keepdims=True))
        p = jnp.exp2((s - m_new).astype(BF16))
        pv = _dot(p, jnp.concatenate([v, jnp.ones_like(v)], axis=1))
        return m_new, acc * jnp.exp2(m - m_new) + pv

    carry = (jnp.full((2 * tq, 1), -1e30, F32), jnp.zeros((2 * tq, 2 * DV_A), F32))
    pending = [(_dot_nt(stack(q), kc_ref[...]), vc_ref[...])]
    if n_kblk:
        q_rot = stack(q * cos_ref[...] + _swap16(q) * sin_ref[...])
        for j in range(n_kblk):
            pending.append((_dot_nt(q_rot, kl_ref[j * TK:(j + 1) * TK, :]), vl_ref[j * TK:(j + 1) * TK, :]))
            if len(pending) > AHEAD:
                carry = update(*pending.pop(0), *carry)
    for s, v in pending:
        carry = update(s, v, *carry)
    _, acc = carry
    o = acc[:, :DV_A] / acc[:, DV_A:DV_A + 1]
    o = o[:tq] - lam_ref[0, 0] * o[tq:]
    o = o * lax.rsqrt(jnp.mean(o * o, axis=-1, keepdims=True) + HEAD_NORM_EPS)
    o_ref[...] = (o * (g_ref[...] * out_scale)).astype(o_ref.dtype)


def _attention(lam, q, k, v, cos_q, sin_q, subln, st, out_scale, latent):
    B, T, Lc = st.B, st.T, st.Lc
    ctx_blk0 = B * T // Lc
    tq = min(TQ, T if latent else Lc)
    n_q = (T if latent else Lc) // tq
    q_row0 = 0 if latent else B * T // tq
    n_kblk = T // TK if latent else 0
    in_specs = [
        pl.BlockSpec(memory_space=pltpu.SMEM),
        pl.BlockSpec((tq, LANES), lambda b, h, i: (q_row0 + b * n_q + i, h)),
        pl.BlockSpec((tq, LANES), lambda b, h, i: (i if latent else 0, 0)),
        pl.BlockSpec((tq, LANES), lambda b, h, i: (i if latent else 0, 0)),
        pl.BlockSpec((1, LANES), lambda b, h, i: (0, 0)),
        pl.BlockSpec((Lc, LANES), lambda b, h, i: (ctx_blk0 + b, h)),
        pl.BlockSpec((Lc, LANES), lambda b, h, i: (ctx_blk0 + b, h)),
    ]
    args = [lam, q, cos_q, sin_q, subln, k, v]
    if latent:
        in_specs += [pl.BlockSpec((T, LANES), lambda b, h, i: (b, h)),
                     pl.BlockSpec((T, LANES), lambda b, h, i: (b, h))]
        args += [k, v]
    return pl.pallas_call(
        functools.partial(_attn_kernel, n_kblk=n_kblk, out_scale=out_scale),
        grid=(B, HA, n_q),
        in_specs=in_specs,
        out_specs=pl.BlockSpec((tq, LANES), lambda b, h, i: (b * n_q + i, h)),
        out_shape=jax.ShapeDtypeStruct((B * n_q * tq, D_A), BF16),
        compiler_params=_cparams(("arbitrary", "arbitrary", "arbitrary")),
        name="attn_lat" if latent else "attn_ctx",
    )(*args)


def _dft2_kernel(a_ref, z_ref, o_ref, acc_ref, *, nk, scale):
    k = pl.program_id(2)

    @pl.when(k == 0)
    def _():
        acc_ref[...] = jnp.zeros_like(acc_ref)

    acc_ref[...] += _dot(a_ref[...], z_ref[0])

    @pl.when(k == nk - 1)
    def _():
        o_ref[...] = (acc_ref[...] * scale).astype(o_ref.dtype)


def _dft2(a_mat, z, n_seq, t_len, row0):
    tm = min(1024, t_len)
    tk = min(2048, t_len)
    kb = t_len // tk
    nk = 2 * kb
    blk0 = row0 // tk
    scale = 1.0 / math.sqrt(t_len * F_GROUP_W)
    return pl.pallas_call(
        functools.partial(_dft2_kernel, nk=nk, scale=scale),
        grid=(n_seq, t_len // tm, nk),
        in_specs=[pl.BlockSpec((tm, tk), lambda b, i, k: (i, k)),
                  pl.BlockSpec((1, tk, D_F), lambda b, i, k: (k // kb, blk0 + b * kb + k % kb, 0))],
        out_specs=pl.BlockSpec((tm, D_F), lambda b, i, k: (b * (t_len // tm) + i, 0)),
        out_shape=jax.ShapeDtypeStruct((n_seq * t_len, D_F), BF16),
        scratch_shapes=[pltpu.VMEM((tm, D_F), F32)],
        compiler_params=_cparams(("arbitrary", "arbitrary", "arbitrary")),
        name="dft_positions",
    )(a_mat, z)


def _dft_tables(t_len):
    n_hi = t_len // LANES
    k = jnp.arange(t_len, dtype=jnp.int32)[:, None]
    unit = 2.0 * math.pi / t_len
    a_hi = ((k * (jnp.arange(n_hi, dtype=jnp.int32) * LANES)[None, :]) % t_len).astype(F32) * unit
    a_lo = ((k * jnp.arange(LANES, dtype=jnp.int32)[None, :]) % t_len).astype(F32) * unit

    def table_kernel(c1_ref, s1_ref, c2_ref, s2_ref, o_ref):
        c2, s2 = c2_ref[...], s2_ref[...]
        for hi in range(n_hi):
            c1, s1 = c1_ref[:, hi:hi + 1], s1_ref[:, hi:hi + 1]
            o_ref[:, hi * LANES:(hi + 1) * LANES] = (c1 * c2 - s1 * s2).astype(BF16)
            o_ref[:, t_len + hi * LANES:t_len + (hi + 1) * LANES] = (-(s1 * c2 + c1 * s2)).astype(BF16)

    row = lambda i: (i, 0)
    return pl.pallas_call(
        table_kernel,
        grid=(t_len // TM,),
        in_specs=[pl.BlockSpec((TM, n_hi), row), pl.BlockSpec((TM, n_hi), row),
                  pl.BlockSpec((TM, LANES), row), pl.BlockSpec((TM, LANES), row)],
        out_specs=pl.BlockSpec((TM, 2 * t_len), row),
        out_shape=jax.ShapeDtypeStruct((t_len, 2 * t_len), BF16),
        compiler_params=_cparams(("arbitrary",)),
        name="dft_table",
    )(jnp.cos(a_hi), jnp.sin(a_hi), jnp.cos(a_lo), jnp.sin(a_lo))


def _rwkv_prep_kernel(cs_ref, csp_ref, csn_ref, co_ref, cop_ref, con_ref, mucs_ref, muco_ref,
                      w0_ref, w2_ref, a0_ref, a2_ref, g2_ref, kk_ref, ka_ref, rk_ref, bd_ref,
                      lw_ref, kd_ref, bb_ref, v_ref, kn_ref, r_ref, bonus_ref, g_ref, *, st):
    i = pl.program_id(0)
    first, last = st.seq_pos(i)

    def tshift(x_ref, p_ref, n_ref, mu_ref):
        x = x_ref[...]
        prev_row = jnp.where(first, 0.0, p_ref[HALO - 1:HALO, :])
        next_row = jnp.where(last, 0.0, n_ref[0:1, :])
        prev, nxt = _shift_rows(x, prev_row, next_row)
        return x + mu_ref[0:1, :] * (prev - x) + mu_ref[1:2, :] * (nxt - x)

    cs = tshift(cs_ref, csp_ref, csn_ref, mucs_ref)
    co = tshift(co_ref, cop_ref, con_ref, muco_ref)
    k = cs[:, :D_C]
    v = cs[:, D_C:2 * D_C]
    wd = cs[:, 2 * D_C:2 * D_C + 2 * LORA_W]
    ad = cs[:, 2 * D_C + 2 * LORA_W:]
    r = co[:, :D_C]
    gd = co[:, D_C:]
    bd = bd_ref[...]

    zw = _dot3w(jnp.tanh(wd), w2_ref)
    za = _dot3w(ad, a2_ref)
    kx = k * kk_ref[...]
    ss = _dot_exact_rhs(kx * kx, bd)
    kn = kx / jnp.maximum(jnp.sqrt(ss), 1e-12)
    kd_sum = jnp.zeros_like(k)
    for d in range(2):
        sl = slice(d * D_C, (d + 1) * D_C)
        z = -(w0_ref[d:d + 1, :] + zw[:, sl])
        softplus = jnp.maximum(z, 0.0) + jnp.log(1.0 + jnp.exp(-jnp.abs(z)))
        lw_ref[d] = -jnp.exp(-softplus - 0.5)
        a = _sigmoid(a0_ref[d:d + 1, :] + za[:, sl])
        kd = k * (1.0 + (a - 1.0) * ka_ref[...])
        kd_ref[d] = kd
        bb_ref[d] = kn * a
        kd_sum = kd_sum + kd
    v_ref[...] = v
    kn_ref[...] = kn
    r_ref[...] = r
    bonus_ref[...] = _dot_exact_rhs(r * kd_sum * rk_ref[...], bd) * v
    g_ref[...] = _dot3w(_sigmoid(gd), g2_ref)


def _rwkv_prep(cs, co, lp, st):
    n = st.n_rows
    row = lambda i: (i, 0)
    full = lambda shape: pl.BlockSpec(shape, lambda i: (0,) * len(shape))
    csp, csn = _halo_specs(D_CS, 0, n)
    cop, con = _halo_specs(D_CO, 0, n)
    dir_out = pl.BlockSpec((2, TM, D_C), lambda i: (0, i, 0))
    tok_out = pl.BlockSpec((TM, D_C), row)
    return pl.pallas_call(
        functools.partial(_rwkv_prep_kernel, st=st),
        grid=(st.tiles,),
        in_specs=[pl.BlockSpec((TM, D_CS), row), csp, csn, pl.BlockSpec((TM, D_CO), row), cop, con,
                  full((2, D_CS)), full((2, D_CO)), full((2, D_C)), full((2, 2 * LORA_W, 2 * D_C)),
                  full((2, D_C)), full((2, 2 * LORA_A, 2 * D_C)), full((2, LORA_G, D_C)),
                  full((1, D_C)), full((1, D_C)), full((1, D_C)), full((D_C, D_C))],
        out_specs=[dir_out, dir_out, dir_out, tok_out, tok_out, tok_out, tok_out, tok_out],
        out_shape=[jax.ShapeDtypeStruct((2, n, D_C), F32)] * 3 + [jax.ShapeDtypeStruct((n, D_C), F32)] * 5,
        compiler_params=_cparams(("arbitrary",)),
        name="rwkv_prep",
    )(cs, cs, cs, co, co, co, lp["mu_cs"], lp["mu_co"], lp["w0"], lp["w2bd"], lp["a0"], lp["a2bd"],
      lp["g2"], lp["k_k"], lp["k_a"], lp["r_k"], lp["head_ones"])


def _rwkv_chunk_kernel(lw_ref, kd_ref, bb_ref, v_ref, kn_ref, r_ref, qh_ref, y0_ref, g_ref, j_ref):
    d = pl.program_id(0)
    L = CHUNK
    ti = lax.broadcasted_iota(jnp.int32, (L, L), 0)
    si = lax.broadcasted_iota(jnp.int32, (L, L), 1)
    sgn = jnp.where(d == 0, 1, -1)
    incl = jnp.where((ti - si) * sgn >= 0, 1.0, 0.0).astype(BF16)
    ri = lax.broadcasted_iota(jnp.int32, (PAIR, PAIR), 0)
    ci = lax.broadcasted_iota(jnp.int32, (PAIR, PAIR), 1)
    same_head = (ri // HS_C) == (ci // HS_C)
    lag = jnp.where(same_head, (ri % L - ci % L) * sgn, -1)
    m_strict = lag > 0
    m_incl = lag >= 0
    eye = (ri == ci).astype(F32)
    lane = lax.broadcasted_iota(jnp.int32, (L, D_C), 1)
    head0 = (lane % PAIR) < HS_C

    def stack(x):
        return jnp.concatenate([jnp.where(head0, x, 0.0), jnp.where(head0, 0.0, x)], axis=0)

    def fold(x):
        return x[:L] + x[L:]

    pairs = range(N_PAIR)
    sls = [slice(p * PAIR, (p + 1) * PAIR) for p in pairs]
    H = PAIR
    units = [(c, p) for c in range(CPS) for p in pairs]

    def wide(c):
        rs = slice(c * L, (c + 1) * L)
        lw, kd, bb = lw_ref[0, rs, :], kd_ref[0, rs, :], bb_ref[0, rs, :]
        v, kn, r = v_ref[rs, :], kn_ref[rs, :], r_ref[rs, :]
        cum = _dot_exact_rhs_lhs(incl, lw)
        tot = jnp.sum(lw, axis=0, keepdims=True)
        e_inv = jnp.exp(-cum)
        e_end = jnp.exp(tot - cum)
        xr_f = stack(r * jnp.exp(cum))
        a_t = -bb * e_inv
        k_t = kd * e_inv
        return dict(
            e_tot=jnp.exp(tot), xr_f=xr_f,
            rows=jnp.concatenate([stack(kn * jnp.exp(cum - lw)), xr_f], axis=0).astype(BF16),
            cols=jnp.concatenate([a_t, a_t, k_t, k_t], axis=0).astype(BF16),
            vs=stack(v).astype(BF16),
            ends=jnp.concatenate([-bb * e_end, kd * e_end], axis=0).astype(BF16),
            v_b=v.astype(BF16))

    ch = [wide(c) for c in range(CPS)]
    big = [_dot_nt(ch[c]["rows"][:, sls[p]], ch[c]["cols"][:, sls[p]]) for c, p in units]
    n_mat = [jnp.where(m_strict, x[:H, :H], 0.0) for x in big]
    m_bk = [jnp.where(m_strict, x[:H, H:], 0.0).astype(BF16) for x in big]
    m_ra = [jnp.where(m_incl, x[H:, :H], 0.0).astype(BF16) for x in big]
    m_rk = [jnp.where(m_incl, x[H:, H:], 0.0).astype(BF16) for x in big]
    n_u = range(len(units))
    u1 = [_dot(m_bk[u], ch[c]["vs"][:, sls[p]]) for u, (c, p) in enumerate(units)]
    t_mat = [eye + x for x in n_mat]
    n_b = [x.astype(BF16) for x in n_mat]
    n_pow = [_dot(x, x).astype(BF16) for x in n_b]
    span = 2
    while span < L // 2:
        res = [_dot(jnp.concatenate([n_pow[u], t_mat[u].astype(BF16)], axis=0), n_pow[u]) for u in n_u]
        n_pow = [x[:H].astype(BF16) for x in res]
        t_mat = [t_mat[u] + res[u][H:] for u in n_u]
        span *= 2
    t_mat = [t_mat[u] + _dot(t_mat[u].astype(BF16), n_pow[u]) for u in n_u]
    wu = [_dot(t_mat[u].astype(BF16), jnp.concatenate([ch[c]["rows"][:H, sls[p]], u1[u].astype(BF16)], axis=1))
          for u, (c, p) in enumerate(units)]
    qy = [_dot(m_ra[u], wu[u].astype(BF16)) for u in n_u]
    y0x = [_dot(m_rk[u], ch[c]["vs"][:, sls[p]]) for u, (c, p) in enumerate(units)]
    zeros = jnp.zeros((L, H), BF16)
    gj = [_dot_tn(ch[c]["ends"][:, sls[p]],
                  jnp.concatenate([fold(wu[u]).astype(BF16),
                                   jnp.concatenate([zeros, ch[c]["v_b"][:, sls[p]]], axis=1)], axis=0))
          for u, (c, p) in enumerate(units)]
    for u, (c, p) in enumerate(units):
        pos = jnp.where(d == 0, c, CPS - 1 - c)
        qh_ref[0, pos, 0, :, sls[p]] = fold(ch[c]["xr_f"][:, sls[p]] + qy[u][:, :H]).astype(qh_ref.dtype)
        y0_ref[0, pos, 0, :, sls[p]] = fold(qy[u][:, H:] + y0x[u]).astype(y0_ref.dtype)
        g_ref[0, pos, 0, p] = fold(jnp.where(same_head, gj[u][:, :H], 0.0) + eye * ch[c]["e_tot"][:, sls[p]])
        j_ref[0, pos, 0, p] = fold(jnp.where(same_head, gj[u][:, H:], 0.0))


def _dot_exact_rhs_lhs(mask_bf16, x):
    x1 = x.astype(BF16)
    r1 = x - x1.astype(F32)
    x2 = r1.astype(BF16)
    x3 = (r1 - x2.astype(F32)).astype(BF16)
    return _dot(mask_bf16, x1) + (_dot(mask_bf16, x2) + _dot(mask_bf16, x3))


def _scan_geometry(st):
    nc_ctx = st.Lc // CHUNK
    nc_lat = st.T // CHUNK
    ns = nc_ctx + nc_lat
    ctx_base = st.n_lat_rows // CHUNK

    def row_block(d, b, s):
        in_ctx = s < nc_ctx
        j_ctx = jnp.where(d == 0, s, nc_ctx - 1 - s)
        j_lat = jnp.where(d == 0, s - nc_ctx, nc_lat - 1 - (s - nc_ctx))
        return jnp.where(in_ctx, ctx_base + b * nc_ctx + j_ctx, b * nc_lat + j_lat)

    return nc_ctx, nc_lat, ns, row_block


def _rwkv_chunks(lw, kd, bb, v, kn, r, st):
    B = st.B
    nc_ctx, nc_lat, ns, row_block = _scan_geometry(st)
    assert nc_ctx % CPS == 0 and nc_lat % CPS == 0
    rows = CPS * CHUNK
    blk = lambda d, b, s: row_block(d, b, s * CPS) // CPS
    dir_in = pl.BlockSpec((1, rows, D_C), lambda d, b, s: (d, blk(d, b, s), 0))
    tok_in = pl.BlockSpec((rows, D_C), lambda d, b, s: (blk(d, b, s), 0))
    row_out = pl.BlockSpec((1, CPS, 1, CHUNK, D_C), lambda d, b, s: (d, s, b, 0, 0))
    mat_out = pl.BlockSpec((1, CPS, 1, N_PAIR, HS_C, PAIR), lambda d, b, s: (d, s, b, 0, 0, 0))
    return pl.pallas_call(
        _rwkv_chunk_kernel,
        grid=(2, B, ns // CPS),
        in_specs=[dir_in, dir_in, dir_in, tok_in, tok_in, tok_in],
        out_specs=[row_out, row_out, mat_out, mat_out],
        out_shape=[jax.ShapeDtypeStruct((2, ns, B, CHUNK, D_C), BF16)] * 2
        + [jax.ShapeDtypeStruct((2, ns, B, N_PAIR, HS_C, PAIR), F32)] * 2,
        compiler_params=_cparams(("arbitrary", "arbitrary", "arbitrary")),
        name="rwkv_chunks",
    )(lw, kd, bb, v, kn, r)


def _rwkv_carry_kernel(qh_ref, y0_ref, g_ref, j_ref, y_ref, h_ref):
    s = pl.program_id(0)

    @pl.when(s == 0)
    def _():
        h_ref[...] = jnp.zeros_like(h_ref)

    nb = qh_ref.shape[2]
    ri = lax.broadcasted_iota(jnp.int32, (PAIR, PAIR), 0)
    ci = lax.broadcasted_iota(jnp.int32, (PAIR, PAIR), 1)
    same_head = (ri // HS_C) == (ci // HS_C)

    def unfold(x):
        return jnp.where(same_head, jnp.concatenate([x, x], axis=0), 0.0)

    for d in range(2):
        for b in range(nb):
            for p in range(N_PAIR):
                sl = slice(p * PAIR, (p + 1) * PAIR)
                hh, hl = _split2(h_ref[d, b, p])
                gh, gl = _split2(unfold(g_ref[d, 0, b, p]))
                top = _dot(jnp.concatenate([qh_ref[d, 0, b, :, sl], gh], axis=0), hh)
                y_ref[d, 0, b, :, sl] = (top[:CHUNK] + y0_ref[d, 0, b, :, sl].astype(F32)).astype(y_ref.dtype)
                h_ref[d, b, p] = top[CHUNK:] + (_dot(gh, hl) + _dot(gl, hh)) + unfold(j_ref[d, 0, b, p])


def _rwkv_carry(qh, y0, g, j, st):
    B = st.B
    ns = qh.shape[1]
    rows = pl.BlockSpec((2, 1, B, CHUNK, D_C), lambda s: (0, s, 0, 0, 0))
    mats = pl.BlockSpec((2, 1, B, N_PAIR, HS_C, PAIR), lambda s: (0, s, 0, 0, 0, 0))
    return pl.pallas_call(
        _rwkv_carry_kernel,
        grid=(ns,),
        in_specs=[rows, rows, mats, mats],
        out_specs=rows,
        out_shape=jax.ShapeDtypeStruct((2, ns, B, CHUNK, D_C), BF16),
        scratch_shapes=[pltpu.VMEM((2, B, N_PAIR, PAIR, PAIR), F32)],
        compiler_params=_cparams(("arbitrary",)),
        name="rwkv_carry",
    )(qh, y0, g, j)


def _mix_out_kernel(*refs, n_src, lat_tiles, alpha):
    o_a = _stream_tile(refs[0:n_src], lat_tiles)
    o_f = _stream_tile(refs[n_src:2 * n_src], lat_tiles)
    x = _stream_tile(refs[2 * n_src:3 * n_src], lat_tiles)
    (yf_ref, yb_ref, bonus_ref, rg_ref, gate_ref, mod_ref, gnw_ref, gnb_ref, bd_ref, wpa_ref, wpf_ref, wpc_ref, wo_ref,
     lng_ref, lnb_ref, x1_ref, h2_ref) = refs[3 * n_src:]
    n = TM // CHUNK
    y = jnp.concatenate([yf_ref[0, c, 0].astype(F32) + yb_ref[0, n - 1 - c, 0].astype(F32) for c in range(n)], axis=0)
    bd = bd_ref[...]
    mu = _dot_exact_rhs(y, bd) * (1.0 / HS_C)
    yc = y - mu
    var = _dot_exact_rhs(yc * yc, bd) * (1.0 / HS_C)
    yn = yc * lax.rsqrt(var + GN_EPS) * gnw_ref[...] + gnb_ref[...]
    o_c = ((yn + bonus_ref[...]) * rg_ref[...]).astype(BF16)
    g0 = gate_ref[:, 0:D_MODEL].astype(F32)
    g1 = gate_ref[:, D_MODEL:2 * D_MODEL].astype(F32)
    g2 = gate_ref[:, 2 * D_MODEL:].astype(F32)
    y = g0 * _dot(o_a, wpa_ref[...]) + g1 * _dot(o_f, wpf_ref[...]) + g2 * _dot(o_c, wpc_ref[...])
    mix = _dot(y.astype(BF16), wo_ref[...])
    z = alpha * x + mod_ref[0, 2:3, :] * mix
    x1 = _ln_plain(z) * lng_ref[...] + lnb_ref[...]
    x1_ref[...] = x1
    h2_ref[...] = (_ln_plain(x1) * (1.0 + mod_ref[0, 4:5, :]) + mod_ref[0, 3:4, :]).astype(BF16)


def _mix_out(o_a, o_f, y, bonus, rg, gate, x_all, mod, lp, st, n_tiles, alpha):
    nc_ctx, nc_lat, _, _ = _scan_geometry(st)
    n = TM // CHUNK
    assert nc_ctx % n == 0

    def seq_of(i):
        lat = i < st.lat_tiles
        b = jnp.where(lat, i // st.tpb, (i - st.lat_tiles) // st.tpc)
        j = jnp.where(lat, i % st.tpb, (i - st.lat_tiles) % st.tpc)
        return lat, b, j

    def fwd(i):
        lat, b, j = seq_of(i)
        return (0, jnp.where(lat, nc_ctx // n + j, j), b, 0, 0)

    def bwd(i):
        lat, b, j = seq_of(i)
        return (1, jnp.where(lat, nc_ctx // n + (st.tpb - 1 - j), st.tpc - 1 - j), b, 0, 0)

    row = lambda i: (i, 0)
    full = lambda shape: pl.BlockSpec(shape, lambda i: (0,) * len(shape))
    o_a, o_f, x_all = _pair(o_a), _pair(o_f), _pair(x_all)
    assert len(o_a) == len(o_f) == len(x_all)
    return pl.pallas_call(
        functools.partial(_mix_out_kernel, n_src=len(x_all), lat_tiles=st.lat_tiles, alpha=alpha),
        grid=(n_tiles,),
        in_specs=_stream_specs(o_a, TM, D_A, st.lat_tiles) + _stream_specs(o_f, TM, D_F, st.lat_tiles)
        + _stream_specs(x_all, TM, D_MODEL, st.lat_tiles)
        + [pl.BlockSpec((1, n, 1, CHUNK, D_C), fwd), pl.BlockSpec((1, n, 1, CHUNK, D_C), bwd),
                  pl.BlockSpec((TM, D_C), row), pl.BlockSpec((TM, D_C), row),
                  pl.BlockSpec((TM, N_BRANCH * D_MODEL), row),
                  pl.BlockSpec((1, 6, D_MODEL), lambda i: (st.mod_index(i), 0, 0)),
                  full((1, D_C)), full((1, D_C)), full((D_C, D_C)),
                  full((D_A, D_MODEL)), full((D_F, D_MODEL)), full((D_C, D_MODEL)), full((D_MODEL, D_MODEL)),
                  full((1, D_MODEL)), full((1, D_MODEL))],
        out_specs=[pl.BlockSpec((TM, D_MODEL), row), pl.BlockSpec((TM, D_MODEL), row)],
        out_shape=[jax.ShapeDtypeStruct((n_tiles * TM, D_MODEL), F32),
                   jax.ShapeDtypeStruct((n_tiles * TM, D_MODEL), BF16)],
        compiler_params=_cparams(("arbitrary",)),
        name="mix_out",
    )(*o_a, *o_f, *x_all, y, y, bonus, rg, gate, mod, lp["ln_w"], lp["ln_b"], lp["head_ones"],
      lp["w_pa"], lp["w_pf"], lp["w_pc"], lp["w_o"], lp["ln1_g"], lp["ln1_b"])


def _ffn_down_kernel(u_ref, g_ref, gp_ref, gn_ref, x_ref, mod_ref, cw_ref, cb_ref, wd_ref, lng_ref, lnb_ref,
                     o_ref, *, st, alpha):
    i = pl.program_id(0)
    first, last = st.seq_pos(i)
    g = g_ref[...].astype(F32)
    prev_row = jnp.where(first, 0.0, gp_ref[HALO - 1:HALO, :].astype(F32))
    next_row = jnp.where(last, 0.0, gn_ref[0:1, :].astype(F32))
    prev, nxt = _shift_rows(g, prev_row, next_row)
    cw = cw_ref[...] * 0.5
    gh = cw[0:1, :] * prev + cw[1:2, :] * g + cw[2:3, :] * nxt + cb_ref[...] * 0.5
    act = gh * (1.0 + lax.erf(gh * (2.0 ** 0.5)))
    a = (act * u_ref[...].astype(F32)).astype(BF16)
    z = alpha * x_ref[...] + mod_ref[0, 5:6, :] * _dot(a, wd_ref[...])
    o_ref[...] = _ln_plain(z) * lng_ref[...] + lnb_ref[...]


def _ffn_down(ug, x1, mod, lp, st, n_tiles, alpha):
    row = lambda i: (i, 0)
    full = lambda shape: pl.BlockSpec(shape, lambda i: (0,) * len(shape))
    gp, gn = _halo_specs(D_FF, 1, ug.shape[0])
    return pl.pallas_call(
        functools.partial(_ffn_down_kernel, st=st, alpha=alpha),
        grid=(n_tiles,),
        in_specs=[pl.BlockSpec((TM, D_FF), lambda i: (i, 0)), pl.BlockSpec((TM, D_FF), lambda i: (i, 1)), gp, gn,
                  pl.BlockSpec((TM, D_MODEL), row),
                  pl.BlockSpec((1, 6, D_MODEL), lambda i: (st.mod_index(i), 0, 0)),
                  full((3, D_FF)), full((1, D_FF)), full((D_FF, D_MODEL)), full((1, D_MODEL)), full((1, D_MODEL))],
        out_specs=pl.BlockSpec((TM, D_MODEL), row),
        out_shape=jax.ShapeDtypeStruct((n_tiles * TM, D_MODEL), F32),
        compiler_params=_cparams(("arbitrary",)),
        name="ffn_down",
    )(ug, ug, ug, ug, x1, mod, lp["ffn_conv"], lp["ffn_conv_b"], lp["ffn_down"], lp["ln2_g"], lp["ln2_b"])


def _rope_tables(st):
    T = st.T
    pos = jnp.arange(T)
    rowp = (pos // GRID_W).astype(F32)
    colp = (pos % GRID_W).astype(F32)
    n_freq = DH_A // 4
    inv = ROPE_BASE ** (-jnp.arange(n_freq, dtype=F32) / n_freq)
    ar, ac = rowp[:, None] * inv, colp[:, None] * inv
    cos64 = jnp.concatenate([jnp.cos(ar), jnp.cos(ar), jnp.cos(ac), jnp.cos(ac)], axis=1)
    sin64 = jnp.concatenate([-jnp.sin(ar), jnp.sin(ar), -jnp.sin(ac), jnp.sin(ac)], axis=1)
    cos_q = jnp.concatenate([cos64, cos64], axis=1)
    sin_q = jnp.concatenate([sin64, sin64], axis=1)
    return cos_q, sin_q


def _block_diag2(m):
    z = jnp.zeros_like(m[0])
    return jnp.concatenate([jnp.concatenate([m[0], z], axis=1), jnp.concatenate([z, m[1]], axis=1)], axis=0)


def _hi_lo(w):
    return jnp.stack(_split2(w))


def kernel(x, c, ctx, c_ctx, ada_w, ada_b, w_in, lam_qk, subln_g, tshift_mu, rw_w0, rw_w2, rw_a0, rw_a2, rw_g2, rw_kk, rw_ka, rw_rk, rw_lnw, rw_lnb, w_pa, w_pf, w_pc, w_o, ln1_g, ln1_b, ffn_up, ffn_conv, ffn_conv_b, ffn_down, ln2_g, ln2_b):
    B, T, D = x.shape
    Lc = ctx.shape[1]
    depth = w_in.shape[0]
    st = _Stream(B, T, Lc)
    alpha = (2.0 * depth) ** 0.25

    cos_q, sin_q = _rope_tables(st)
    dft_lat = _dft_tables(T)
    dft_ctx = _dft_tables(Lc)
    kf = jnp.arange(F_GROUP_W, dtype=jnp.int32)
    ang = ((kf[:, None] * kf[None, :]) % F_GROUP_W).astype(F32) * (2.0 * math.pi / F_GROUP_W)
    dft_ch = jnp.concatenate([jnp.cos(ang), jnp.sin(ang)], axis=1).astype(BF16)
    hi = jnp.arange(D_C) // HS_C
    head_ones = (hi[:, None] == hi[None, :]).astype(BF16)
    c_all = jnp.concatenate([c, c_ctx[None], jnp.zeros((8 - B - 1, D), F32)], axis=0)

    w_qkvf, w_rest = _cast_bf16(w_in, (OFF_CS, D_IN - OFF_CS))
    (w_up,), (w_dn,) = _cast_bf16(ffn_up), _cast_bf16(ffn_down)
    (w_pa_b,), (w_pf_b,), (w_pc_b,), (w_o_b,) = _cast_bf16(w_pa), _cast_bf16(w_pf), _cast_bf16(w_pc), _cast_bf16(w_o)

    x_all = (x.reshape(B * T, D), ctx.reshape(B * Lc, D))
    for i in range(depth):
        last = i == depth - 1
        lam_init = 0.8 - 0.6 * math.exp(-0.3 * i)
        lq = lam_qk[i].astype(F32)
        lam = (jnp.exp(jnp.sum(lq[0] * lq[1])) - jnp.exp(jnp.sum(lq[2] * lq[3])) + lam_init).reshape(1, 1)
        lp = {
            "mu_cs": tshift_mu[i][:, :D_CS], "mu_co": tshift_mu[i][:, D_CS:],
            "w0": rw_w0[i], "w2bd": _hi_lo(_block_diag2(rw_w2[i])), "a0": rw_a0[i],
            "a2bd": _hi_lo(_block_diag2(rw_a2[i])),
            "g2": _hi_lo(rw_g2[i]), "k_k": rw_kk[i].reshape(1, D_C), "k_a": rw_ka[i].reshape(1, D_C),
            "r_k": rw_rk[i].reshape(1, D_C), "ln_w": rw_lnw[i].reshape(1, D_C), "ln_b": rw_lnb[i].reshape(1, D_C),
            "head_ones": head_ones,
            "w_pa": w_pa_b[i], "w_pf": w_pf_b[i], "w_pc": w_pc_b[i], "w_o": w_o_b[i], "ln1_g": ln1_g[i].reshape(1, D), "ln1_b": ln1_b[i].reshape(1, D),
            "ffn_conv": ffn_conv[i], "ffn_conv_b": ffn_conv_b[i].reshape(1, D_FF),
            "ffn_down": w_dn[i], "ln2_g": ln2_g[i].reshape(1, D), "ln2_b": ln2_b[i].reshape(1, D),
        }
        n_out = st.lat_tiles if last else st.tiles

        mod = _adaln(c_all, ada_w[i], ada_b[i]).reshape(8, 6, D)
        q, k, v, z = _proj(x_all, w_qkvf, i, ((D_A, 512, "plain", F32), (D_A, 512, "rope", BF16),
                                                    (D_A, 512, "plain", BF16), (D_F, 512, "dft", BF16)),
                           (cos_q, sin_q, dft_ch), mod, st)
        cs, co, gate = _proj(x_all, w_rest, i, ((D_CS, D_CS // 2, "plain", F32), (D_CO, D_CO, "plain", F32),
                                                       (N_BRANCH * D_MODEL, 512, "sigmoid", BF16)), (), mod, st)

        subln = subln_g[i].reshape(1, DV_A)
        o_a = _attention(lam, q, k, v, cos_q, sin_q, subln, st, 1.0 - lam_init, True)
        o_f = _dft2(dft_lat, z, B, T, 0)
        if not last:
            o_a = (o_a, _attention(lam, q, k, v, cos_q, sin_q, subln, st, 1.0 - lam_init, False))
            o_f = (o_f, _dft2(dft_ctx, z, B, Lc, st.n_lat_rows))
        x_mix = x_all
        if last and isinstance(x_all, tuple):
            x_mix = x_all[0]

        lw, kd, bb, rv, kn, rr, bonus, rg = _rwkv_prep(cs, co, lp, st)
        qh, y0, gm, jm = _rwkv_chunks(lw, kd, bb, rv, kn, rr, st)
        y = _rwkv_carry(qh, y0, gm, jm, st)
        x1, h2 = _mix_out(o_a, o_f, y, bonus, rg, gate, x_mix, mod, lp, st, n_out, alpha)
        (ug,) = _proj(h2, w_up, i, ((2 * D_FF, 512, "plain", BF16),))
        x_all = _ffn_down(ug, x1, mod, lp, st, n_out, alpha)
    return x_all[:B * T].reshape(B, T, D)
```

```python
import functools
import math

import jax
import jax.numpy as jnp
from jax import lax
from jax.experimental import pallas as pl
from jax.experimental.pallas import tpu as pltpu

F32 = jnp.float32
BF16 = jnp.bfloat16

D_MODEL = 1024
DEPTH = 2
GRID_W = 64
HA = 8
DH_A = 64
DV_A = 2 * DH_A
D_A = HA * DV_A
ROPE_BASE = 10000.0
F_GROUPS = 4
F_GROUP_W = 128
D_F = F_GROUPS * F_GROUP_W
HC = 8
HS_C = 64
D_C = HC * HS_C
LORA_W = 64
LORA_A = 64
LORA_G = 128
D_CS = 2 * D_C + 2 * LORA_W + 2 * LORA_A
D_CO = D_C + LORA_G
N_BRANCH = 3
D_FF = 2816
LN_EPS = 1e-5
GN_EPS = HS_C * 1e-5
HEAD_NORM_EPS = 1e-5

OFF_Q = 0
OFF_K = OFF_Q + D_A
OFF_V = OFF_K + D_A
OFF_F = OFF_V + D_A
OFF_CS = OFF_F + D_F
OFF_CO = OFF_CS + D_CS
OFF_GATE = OFF_CO + D_CO
D_IN = OFF_GATE + N_BRANCH * D_MODEL

LANES = 128
HALO = 16
TM = 256
TMP = 512
CHUNK = 64
PAIR = 2 * HS_C
CPS = 4
N_PAIR = D_C // PAIR
VMEM_LIMIT = 56 * 1024 * 1024


def _cparams(sem):
    return pltpu.CompilerParams(dimension_semantics=sem, vmem_limit_bytes=VMEM_LIMIT)


def _split2(x):
    hi = x.astype(BF16)
    lo = (x - hi.astype(F32)).astype(BF16)
    return hi, lo


def _dot(a, b, dims=(((1,), (0,)), ((), ()))):
    return lax.dot_general(a, b, dims, preferred_element_type=F32)


def _dot_nt(a, b):
    return lax.dot_general(a, b, (((1,), (1,)), ((), ())), preferred_element_type=F32)


def _dot_tn(a, b):
    return lax.dot_general(a, b, (((0,), (0,)), ((), ())), preferred_element_type=F32)


def _dotb(a, b):
    return _dot(a.astype(BF16), b.astype(BF16))


def _dot3(a, b):
    ah, al = _split2(a)
    bh, bl = _split2(b)
    return _dot(ah, bh) + (_dot(ah, bl) + _dot(al, bh))


def _dot3w(a, w_ref):
    ah, al = _split2(a)
    return _dot(ah, w_ref[0]) + (_dot(ah, w_ref[1]) + _dot(al, w_ref[0]))


def _dot_exact_rhs(a, b_bf16):
    a1, a2 = _split2(a)
    return _dot(a1, b_bf16) + _dot(a2, b_bf16)


def _ln_plain(x):
    mu = jnp.mean(x, axis=-1, keepdims=True)
    xc = x - mu
    var = jnp.mean(xc * xc, axis=-1, keepdims=True)
    return xc * lax.rsqrt(var + LN_EPS)


def _sigmoid(x):
    return 1.0 / (1.0 + jnp.exp(-x))


def _swap16(x):
    n = x.shape[-1]
    lane = lax.broadcasted_iota(jnp.int32, x.shape, x.ndim - 1)
    up = pltpu.roll(x, n - 16, axis=x.ndim - 1)
    dn = pltpu.roll(x, 16, axis=x.ndim - 1)
    return jnp.where((lane % 32) < 16, up, dn)


def _shift_rows(x, prev_row, next_row):
    n = x.shape[0]
    row = lax.broadcasted_iota(jnp.int32, x.shape, 0)
    prev = jnp.where(row == 0, prev_row, pltpu.roll(x, 1, axis=0))
    nxt = jnp.where(row == n - 1, next_row, pltpu.roll(x, n - 1, axis=0))
    return prev, nxt


class _Stream:
    def __init__(self, B, T, Lc):
        assert T % TM == 0 and Lc % TM == 0 and (B * T) % Lc == 0
        self.B, self.T, self.Lc = B, T, Lc
        self.n_lat_rows = B * T
        self.n_rows = B * T + B * Lc
        self.lat_tiles = B * T // TM
        self.tiles = self.n_rows // TM
        self.tpb = T // TM
        self.tpc = Lc // TM

    def mod_index(self, i):
        return jnp.where(i < self.lat_tiles, i // self.tpb, self.B)

    def seq_pos(self, i):
        j = jnp.where(i < self.lat_tiles, i % self.tpb, (i - self.lat_tiles) % self.tpc)
        n = jnp.where(i < self.lat_tiles, self.tpb, self.tpc)
        return j == 0, j == n - 1


def _pair(a):
    return a if isinstance(a, tuple) else (a,)


def _stream_specs(parts, tile, width, n_lat_tiles):
    if len(parts) == 1:
        return [pl.BlockSpec((tile, width), lambda i: (i, 0))]
    return [pl.BlockSpec((tile, width), lambda i: (jnp.minimum(i, n_lat_tiles - 1), 0)),
            pl.BlockSpec((tile, width), lambda i: (jnp.maximum(i - n_lat_tiles, 0), 0))]


def _stream_tile(refs, n_lat_tiles):
    if len(refs) == 1:
        return refs[0][...]
    return jnp.where(pl.program_id(0) < n_lat_tiles, refs[0][...], refs[1][...])


def _halo_specs(width, col_block, n_rows):
    per = TM // HALO
    last = n_rows // HALO - 1
    prev = pl.BlockSpec((HALO, width), lambda i: (jnp.maximum(i * per - 1, 0), col_block))
    nxt = pl.BlockSpec((HALO, width), lambda i: (jnp.minimum((i + 1) * per, last), col_block))
    return prev, nxt


def _adaln_kernel(c_ref, w_ref, b_ref, o_ref):
    c = c_ref[...]
    s = c * _sigmoid(c)
    o_ref[...] = _dot3(s, w_ref[...]) + b_ref[...]


def _adaln(c_all, w, b, layer):
    n = w.shape[2]
    tn = 1536
    return pl.pallas_call(
        _adaln_kernel,
        grid=(n // tn,),
        in_specs=[pl.BlockSpec((8, D_MODEL), lambda j: (0, 0)),
                  pl.BlockSpec((None, D_MODEL, tn), lambda j: (layer, 0, j)),
                  pl.BlockSpec((None, 1, tn), lambda j: (layer, 0, j))],
        out_specs=pl.BlockSpec((8, tn), lambda j: (0, j)),
        out_shape=jax.ShapeDtypeStruct((8, n), F32),
        compiler_params=_cparams(("arbitrary",)),
        name="adaln",
    )(c_all, w, b.reshape(b.shape[0], 1, n))


def _cast_kernel(w_ref, *o_refs):
    off = 0
    for o_ref in o_refs:
        width = o_ref.shape[-1]
        o_ref[...] = w_ref[:, :, off:off + width].astype(o_ref.dtype)
        off += width


def _cast_bf16(w, splits=None):
    n_l, rows, cols = w.shape
    splits = splits or (cols,)
    assert sum(splits) == cols and rows % TM == 0
    return pl.pallas_call(
        _cast_kernel,
        grid=(n_l, rows // TM),
        in_specs=[pl.BlockSpec((1, TM, cols), lambda l, r: (l, r, 0))],
        out_specs=[pl.BlockSpec((1, TM, c), lambda l, r: (l, r, 0)) for c in splits],
        out_shape=[jax.ShapeDtypeStruct((n_l, rows, c), BF16) for c in splits],
        compiler_params=_cparams(("arbitrary", "arbitrary")),
        name="cast_weights",
    )(w)


def _proj_kernel(*refs, n_src, groups, has_mod, n_tables, lat_tiles):
    a = _stream_tile(refs[:n_src], lat_tiles)
    w_ref, rest = refs[n_src], refs[n_src + 1:]
    if has_mod:
        mod_ref, rest = rest[0], rest[1:]
        a = (_ln_plain(a) * (1.0 + mod_ref[0, 1:2, :]) + mod_ref[0, 0:1, :]).astype(BF16)
    tables, outs = rest[:n_tables], rest[n_tables:]
    off = 0
    for (width, chunk, epi), o_ref in zip(groups, outs):
        for c0 in range(0, width, chunk):
            acc = _dot(a, w_ref[:, off + c0:off + c0 + chunk])
            if epi == "rope":
                reps = chunk // LANES
                lat = pl.program_id(0) < lat_tiles
                cos = jnp.where(lat, jnp.concatenate([tables[0][...]] * reps, axis=1), 1.0)
                sin = jnp.where(lat, jnp.concatenate([tables[1][...]] * reps, axis=1), 0.0)
                acc = acc * cos + _swap16(acc) * sin
            elif epi == "sigmoid":
                acc = _sigmoid(acc)
            if epi == "dft":
                fb = acc.astype(BF16)
                for g in range(chunk // F_GROUP_W):
                    sl = slice(c0 + g * F_GROUP_W, c0 + (g + 1) * F_GROUP_W)
                    r = _dot(fb[:, g * F_GROUP_W:(g + 1) * F_GROUP_W], tables[2][...])
                    o_ref[0, :, sl] = r[:, :F_GROUP_W].astype(o_ref.dtype)
                    o_ref[1, :, sl] = r[:, F_GROUP_W:].astype(o_ref.dtype)
            else:
                o_ref[:, c0:c0 + chunk] = acc.astype(o_ref.dtype)
        off += width


def _proj(h, w, layer, groups, tables=(), mod=None, st=None):
    srcs = _pair(h)
    m, k = sum(a.shape[0] for a in srcs), srcs[0].shape[1]
    n = w.shape[2]
    assert all(a.shape[0] % TMP == 0 for a in srcs) and sum(g[0] for g in groups) == n
    assert all(g[0] % g[1] == 0 for g in groups)
    row = lambda i: (i, 0)
    lat_tiles, per_seq = (st.n_lat_rows // TMP, st.T // TMP) if st is not None else (0, 1)
    specs = _stream_specs(srcs, TMP, k, lat_tiles) + [pl.BlockSpec((None, k, n), lambda i: (layer, 0, 0))]
    args = list(srcs) + [w]
    if mod is not None:
        specs.append(pl.BlockSpec((1, 6, D_MODEL), lambda i: (jnp.where(i < lat_tiles, i // per_seq, st.B), 0, 0)))
        args.append(mod)
    for t in tables:
        if st is not None and t.shape == (st.T, LANES):
            specs.append(pl.BlockSpec((TMP, LANES), lambda i: (jnp.where(i < lat_tiles, i % per_seq, 0), 0)))
        else:
            specs.append(pl.BlockSpec(t.shape, lambda i: (0, 0)))
    out_specs = [pl.BlockSpec((2, TMP, g[0]), lambda i: (0, i, 0)) if g[2] == "dft" else pl.BlockSpec((TMP, g[0]), row)
                 for g in groups]
    out_shape = [jax.ShapeDtypeStruct((2, m, g[0]) if g[2] == "dft" else (m, g[0]), g[3]) for g in groups]
    return pl.pallas_call(
        functools.partial(_proj_kernel, n_src=len(srcs), groups=tuple(g[:3] for g in groups), has_mod=mod is not None,
                          n_tables=len(tables), lat_tiles=lat_tiles),
        grid=(m // TMP,),
        in_specs=specs,
        out_specs=out_specs,
        out_shape=out_shape,
        compiler_params=_cparams(("arbitrary",)),
        name="proj_" + "_".join(g[2] for g in groups),
    )(*args, *tables)


TQ = 512
TK = 512
AHEAD = 1


def _attn_kernel(lam_ref, q_ref, cos_ref, sin_ref, g_ref, kc_ref, vc_ref, *rest, n_kblk, out_scale):
    if n_kblk:
        kl_ref, vl_ref, o_ref = rest
    else:
        (o_ref,) = rest
    tq = q_ref.shape[0]
    lane = lax.broadcasted_iota(jnp.int32, (tq, LANES), 1)
    first = lane < DH_A
    q = q_ref[...] * (DH_A ** -0.5 * math.log2(math.e))

    def stack(x):
        return jnp.concatenate([jnp.where(first, x, 0.0), jnp.where(first, 0.0, x)], axis=0).astype(BF16)

    def update(s, v, m, acc):
        m_new = jnp.maximum(m, jnp.max(s, axis=-1, keepdims=True))
        p = jnp.exp2((s - m_new).astype(BF16))
        pv = _dot(p, jnp.concatenate([v, jnp.ones_like(v)], axis=1))
        return m_new, acc * jnp.exp2(m - m_new) + pv

    carry = (jnp.full((2 * tq, 1), -1e30, F32), jnp.zeros((2 * tq, 2 * DV_A), F32))
    pending = [(_dot_nt(stack(q), kc_ref[...]), vc_ref[...])]
    if n_kblk:
        q_rot = stack(q * cos_ref[...] + _swap16(q) * sin_ref[...])
        for j in range(n_kblk):
            pending.append((_dot_nt(q_rot, kl_ref[j * TK:(j + 1) * TK, :]), vl_ref[j * TK:(j + 1) * TK, :]))
            if len(pending) > AHEAD:
                carry = update(*pending.pop(0), *carry)
    for s, v in pending:
        carry = update(s, v, *carry)
    _, acc = carry
    o = acc[:, :DV_A] / acc[:, DV_A:DV_A + 1]
    o = o[:tq] - lam_ref[0, 0] * o[tq:]
    o = o * lax.rsqrt(jnp.mean(o * o, axis=-1, keepdims=True) + HEAD_NORM_EPS)
    o_ref[...] = (o * (g_ref[...] * out_scale)).astype(o_ref.dtype)


def _attention(lam, q, k, v, cos_q, sin_q, subln, st, out_scale, latent):
    B, T, Lc = st.B, st.T, st.Lc
    ctx_blk0 = B * T // Lc
    tq = min(TQ, T if latent else Lc)
    n_q = (T if latent else Lc) // tq
    q_row0 = 0 if latent else B * T // tq
    n_kblk = T // TK if latent else 0
    in_specs = [
        pl.BlockSpec(memory_space=pltpu.SMEM),
        pl.BlockSpec((tq, LANES), lambda b, h, i: (q_row0 + b * n_q + i, h)),
        pl.BlockSpec((tq, LANES), lambda b, h, i: (i if latent else 0, 0)),
        pl.BlockSpec((tq, LANES), lambda b, h, i: (i if latent else 0, 0)),
        pl.BlockSpec((1, LANES), lambda b, h, i: (0, 0)),
        pl.BlockSpec((Lc, LANES), lambda b, h, i: (ctx_blk0 + b, h)),
        pl.BlockSpec((Lc, LANES), lambda b, h, i: (ctx_blk0 + b, h)),
    ]
    args = [lam, q, cos_q, sin_q, subln, k, v]
    if latent:
        in_specs += [pl.BlockSpec((T, LANES), lambda b, h, i: (b, h)),
                     pl.BlockSpec((T, LANES), lambda b, h, i: (b, h))]
        args += [k, v]
    return pl.pallas_call(
        functools.partial(_attn_kernel, n_kblk=n_kblk, out_scale=out_scale),
        grid=(B, HA, n_q),
        in_specs=in_specs,
        out_specs=pl.BlockSpec((tq, LANES), lambda b, h, i: (b * n_q + i, h)),
        out_shape=jax.ShapeDtypeStruct((B * n_q * tq, D_A), BF16),
        compiler_params=_cparams(("arbitrary", "arbitrary", "arbitrary")),
        name="attn_lat" if latent else "attn_ctx",
    )(*args)


def _dft2_kernel(a_ref, z_ref, o_ref, acc_ref, *, nk, scale):
    k = pl.program_id(2)

    @pl.when(k == 0)
    def _():
        acc_ref[...] = jnp.zeros_like(acc_ref)

    acc_ref[...] += _dot(a_ref[...], z_ref[0])

    @pl.when(k == nk - 1)
    def _():
        o_ref[...] = (acc_ref[...] * scale).astype(o_ref.dtype)


def _dft2(a_mat, z, n_seq, t_len, row0):
    tm = min(1024, t_len)
    tk = min(2048, t_len)
    kb = t_len // tk
    nk = 2 * kb
    blk0 = row0 // tk
    scale = 1.0 / math.sqrt(t_len * F_GROUP_W)
    return pl.pallas_call(
        functools.partial(_dft2_kernel, nk=nk, scale=scale),
        grid=(n_seq, t_len // tm, nk),
        in_specs=[pl.BlockSpec((tm, tk), lambda b, i, k: (i, k)),
                  pl.BlockSpec((1, tk, D_F), lambda b, i, k: (k // kb, blk0 + b * kb + k % kb, 0))],
        out_specs=pl.BlockSpec((tm, D_F), lambda b, i, k: (b * (t_len // tm) + i, 0)),
        out_shape=jax.ShapeDtypeStruct((n_seq * t_len, D_F), BF16),
        scratch_shapes=[pltpu.VMEM((tm, D_F), F32)],
        compiler_params=_cparams(("arbitrary", "arbitrary", "arbitrary")),
        name="dft_positions",
    )(a_mat, z)


def _dft_tables(t_len):
    n_hi = t_len // LANES
    k = jnp.arange(t_len, dtype=jnp.int32)[:, None]
    unit = 2.0 * math.pi / t_len
    a_hi = ((k * (jnp.arange(n_hi, dtype=jnp.int32) * LANES)[None, :]) % t_len).astype(F32) * unit
    a_lo = ((k * jnp.arange(LANES, dtype=jnp.int32)[None, :]) % t_len).astype(F32) * unit

    def table_kernel(c1_ref, s1_ref, c2_ref, s2_ref, o_ref):
        c2, s2 = c2_ref[...], s2_ref[...]
        for hi in range(n_hi):
            c1, s1 = c1_ref[:, hi:hi + 1], s1_ref[:, hi:hi + 1]
            o_ref[:, hi * LANES:(hi + 1) * LANES] = (c1 * c2 - s1 * s2).astype(BF16)
            o_ref[:, t_len + hi * LANES:t_len + (hi + 1) * LANES] = (-(s1 * c2 + c1 * s2)).astype(BF16)

    row = lambda i: (i, 0)
    return pl.pallas_call(
        table_kernel,
        grid=(t_len // TM,),
        in_specs=[pl.BlockSpec((TM, n_hi), row), pl.BlockSpec((TM, n_hi), row),
                  pl.BlockSpec((TM, LANES), row), pl.BlockSpec((TM, LANES), row)],
        out_specs=pl.BlockSpec((TM, 2 * t_len), row),
        out_shape=jax.ShapeDtypeStruct((t_len, 2 * t_len), BF16),
        compiler_params=_cparams(("arbitrary",)),
        name="dft_table",
    )(jnp.cos(a_hi), jnp.sin(a_hi), jnp.cos(a_lo), jnp.sin(a_lo))


def _rwkv_prep_kernel(cs_ref, csp_ref, csn_ref, co_ref, cop_ref, con_ref, mucs_ref, muco_ref,
                      w0_ref, w2_ref, a0_ref, a2_ref, g2_ref, kk_ref, ka_ref, rk_ref, bd_ref,
                      lw_ref, kd_ref, bb_ref, v_ref, kn_ref, r_ref, bonus_ref, g_ref, *, st):
    i = pl.program_id(0)
    first, last = st.seq_pos(i)

    def tshift(x_ref, p_ref, n_ref, mu_ref):
        x = x_ref[...]
        prev_row = jnp.where(first, 0.0, p_ref[HALO - 1:HALO, :])
        next_row = jnp.where(last, 0.0, n_ref[0:1, :])
        prev, nxt = _shift_rows(x, prev_row, next_row)
        return x + mu_ref[0:1, :] * (prev - x) + mu_ref[1:2, :] * (nxt - x)

    cs = tshift(cs_ref, csp_ref, csn_ref, mucs_ref)
    co = tshift(co_ref, cop_ref, con_ref, muco_ref)
    k = cs[:, :D_C]
    v = cs[:, D_C:2 * D_C]
    wd = cs[:, 2 * D_C:2 * D_C + 2 * LORA_W]
    ad = cs[:, 2 * D_C + 2 * LORA_W:]
    r = co[:, :D_C]
    gd = co[:, D_C:]
    bd = bd_ref[...]

    zw = _dot3w(jnp.tanh(wd), w2_ref)
    za = _dot3w(ad, a2_ref)
    kx = k * kk_ref[...]
    ss = _dot_exact_rhs(kx * kx, bd)
    kn = kx / jnp.maximum(jnp.sqrt(ss), 1e-12)
    kd_sum = jnp.zeros_like(k)
    for d in range(2):
        sl = slice(d * D_C, (d + 1) * D_C)
        z = -(w0_ref[d:d + 1, :] + zw[:, sl])
        softplus = jnp.maximum(z, 0.0) + jnp.log(1.0 + jnp.exp(-jnp.abs(z)))
        lw_ref[d] = -jnp.exp(-softplus - 0.5)
        a = _sigmoid(a0_ref[d:d + 1, :] + za[:, sl])
        kd = k * (1.0 + (a - 1.0) * ka_ref[...])
        kd_ref[d] = kd
        bb_ref[d] = kn * a
        kd_sum = kd_sum + kd
    v_ref[...] = v
    kn_ref[...] = kn
    r_ref[...] = r
    bonus_ref[...] = _dot_exact_rhs(r * kd_sum * rk_ref[...], bd) * v
    g_ref[...] = _dot3w(_sigmoid(gd), g2_ref)


def _rwkv_prep(cs, co, lp, st):
    n = st.n_rows
    row = lambda i: (i, 0)
    full = lambda shape: pl.BlockSpec(shape, lambda i: (0,) * len(shape))
    csp, csn = _halo_specs(D_CS, 0, n)
    cop, con = _halo_specs(D_CO, 0, n)
    dir_out = pl.BlockSpec((2, TM, D_C), lambda i: (0, i, 0))
    tok_out = pl.BlockSpec((TM, D_C), row)
    return pl.pallas_call(
        functools.partial(_rwkv_prep_kernel, st=st),
        grid=(st.tiles,),
        in_specs=[pl.BlockSpec((TM, D_CS), row), csp, csn, pl.BlockSpec((TM, D_CO), row), cop, con,
                  full((2, D_CS)), full((2, D_CO)), full((2, D_C)), full((2, 2 * LORA_W, 2 * D_C)),
                  full((2, D_C)), full((2, 2 * LORA_A, 2 * D_C)), full((2, LORA_G, D_C)),
                  full((1, D_C)), full((1, D_C)), full((1, D_C)), full((D_C, D_C))],
        out_specs=[dir_out, dir_out, dir_out, tok_out, tok_out, tok_out, tok_out, tok_out],
        out_shape=[jax.ShapeDtypeStruct((2, n, D_C), F32)] * 3 + [jax.ShapeDtypeStruct((n, D_C), F32)] * 5,
        compiler_params=_cparams(("arbitrary",)),
        name="rwkv_prep",
    )(cs, cs, cs, co, co, co, lp["mu_cs"], lp["mu_co"], lp["w0"], lp["w2bd"], lp["a0"], lp["a2bd"],
      lp["g2"], lp["k_k"], lp["k_a"], lp["r_k"], lp["head_ones"])


def _rwkv_chunk_kernel(lw_ref, kd_ref, bb_ref, v_ref, kn_ref, r_ref, qh_ref, y0_ref, g_ref, j_ref):
    d = pl.program_id(0)
    L = CHUNK
    sgn = jnp.where(d == 0, 1, -1)
    row_t = lax.broadcasted_iota(jnp.int32, (L, D_C), 0)
    ri = lax.broadcasted_iota(jnp.int32, (PAIR, PAIR), 0)
    ci = lax.broadcasted_iota(jnp.int32, (PAIR, PAIR), 1)
    same_head = (ri // HS_C) == (ci // HS_C)
    lag = jnp.where(same_head, (ri % L - ci % L) * sgn, -1)
    m_strict = lag > 0
    m_incl = lag >= 0
    eye = (ri == ci).astype(F32)
    lane = lax.broadcasted_iota(jnp.int32, (L, D_C), 1)
    head0 = (lane % PAIR) < HS_C

    def stack(x):
        return jnp.concatenate([jnp.where(head0, x, 0.0), jnp.where(head0, 0.0, x)], axis=0)

    def fold(x):
        return x[:L] + x[L:]

    pairs = range(N_PAIR)
    sls = [slice(p * PAIR, (p + 1) * PAIR) for p in pairs]
    H = PAIR
    units = [(c, p) for c in range(CPS) for p in pairs]

    def wide(c):
        rs = slice(c * L, (c + 1) * L)
        lw, kd, bb = lw_ref[0, rs, :], kd_ref[0, rs, :], bb_ref[0, rs, :]
        v, kn, r = v_ref[rs, :], kn_ref[rs, :], r_ref[rs, :]
        tot = jnp.sum(lw, axis=0, keepdims=True)
        pre = lw
        span = 1
        while span < L:
            pre = pre + jnp.where(row_t >= span, pltpu.roll(pre, span, axis=0), 0.0)
            span *= 2
        cum = jnp.where(d == 0, pre, tot - pre + lw)
        e_inv = jnp.exp(-cum)
        e_end = jnp.exp(tot - cum)
        xr_f = stack(r * jnp.exp(cum))
        a_t = -bb * e_inv
        k_t = kd * e_inv
        return dict(
            e_tot=jnp.exp(tot), xr_f=xr_f,
            rows=jnp.concatenate([stack(kn * jnp.exp(cum - lw)), xr_f], axis=0).astype(BF16),
            cols=jnp.concatenate([a_t, a_t, k_t, k_t], axis=0).astype(BF16),
            vs=stack(v).astype(BF16),
            ends=jnp.concatenate([-bb * e_end, kd * e_end], axis=0).astype(BF16),
            v_b=v.astype(BF16))

    ch = [wide(c) for c in range(CPS)]
    big = [_dot_nt(ch[c]["rows"][:, sls[p]], ch[c]["cols"][:, sls[p]]) for c, p in units]
    n_mat = [jnp.where(m_strict, x[:H, :H], 0.0) for x in big]
    m_bk = [jnp.where(m_strict, x[:H, H:], 0.0).astype(BF16) for x in big]
    m_ra = [jnp.where(m_incl, x[H:, :H], 0.0).astype(BF16) for x in big]
    m_rk = [jnp.where(m_incl, x[H:, H:], 0.0).astype(BF16) for x in big]
    n_u = range(len(units))
    u1 = [_dot(m_bk[u], ch[c]["vs"][:, sls[p]]) for u, (c, p) in enumerate(units)]
    t_mat = [eye + x for x in n_mat]
    n_b = [x.astype(BF16) for x in n_mat]
    n_pow = [_dot(x, x).astype(BF16) for x in n_b]
    span = 2
    while span < L // 2:
        res = [_dot(jnp.concatenate([n_pow[u], t_mat[u].astype(BF16)], axis=0), n_pow[u]) for u in n_u]
        n_pow = [x[:H].astype(BF16) for x in res]
        t_mat = [t_mat[u] + res[u][H:] for u in n_u]
        span *= 2
    t_mat = [t_mat[u] + _dot(t_mat[u].astype(BF16), n_pow[u]) for u in n_u]
    wu = [_dot(t_mat[u].astype(BF16), jnp.concatenate([ch[c]["rows"][:H, sls[p]], u1[u].astype(BF16)], axis=1))
          for u, (c, p) in enumerate(units)]
    qy = [_dot(m_ra[u], wu[u].astype(BF16)) for u in n_u]
    y0x = [_dot(m_rk[u], ch[c]["vs"][:, sls[p]]) for u, (c, p) in enumerate(units)]
    zeros = jnp.zeros((L, H), BF16)
    gj = [_dot_tn(ch[c]["ends"][:, sls[p]],
                  jnp.concatenate([fold(wu[u]).astype(BF16),
                                   jnp.concatenate([zeros, ch[c]["v_b"][:, sls[p]]], axis=1)], axis=0))
          for u, (c, p) in enumerate(units)]
    for u, (c, p) in enumerate(units):
        pos = jnp.where(d == 0, c, CPS - 1 - c)
        qh_ref[0, pos, 0, :, sls[p]] = fold(ch[c]["xr_f"][:, sls[p]] + qy[u][:, :H]).astype(qh_ref.dtype)
        y0_ref[0, pos, 0, :, sls[p]] = fold(qy[u][:, H:] + y0x[u]).astype(y0_ref.dtype)
        g_ref[0, pos, 0, p] = fold(jnp.where(same_head, gj[u][:, :H], 0.0) + eye * ch[c]["e_tot"][:, sls[p]])
        j_ref[0, pos, 0, p] = fold(jnp.where(same_head, gj[u][:, H:], 0.0))


def _scan_geometry(st):
    nc_ctx = st.Lc // CHUNK
    nc_lat = st.T // CHUNK
    ns = nc_ctx + nc_lat
    ctx_base = st.n_lat_rows // CHUNK

    def row_block(d, b, s):
        in_ctx = s < nc_ctx
        j_ctx = jnp.where(d == 0, s, nc_ctx - 1 - s)
        j_lat = jnp.where(d == 0, s - nc_ctx, nc_lat - 1 - (s - nc_ctx))
        return jnp.where(in_ctx, ctx_base + b * nc_ctx + j_ctx, b * nc_lat + j_lat)

    return nc_ctx, nc_lat, ns, row_block


def _rwkv_chunks(lw, kd, bb, v, kn, r, st):
    B = st.B
    nc_ctx, nc_lat, ns, row_block = _scan_geometry(st)
    assert nc_ctx % CPS == 0 and nc_lat % CPS == 0
    rows = CPS * CHUNK
    blk = lambda d, b, s: row_block(d, b, s * CPS) // CPS
    dir_in = pl.BlockSpec((1, rows, D_C), lambda d, b, s: (d, blk(d, b, s), 0))
    tok_in = pl.BlockSpec((rows, D_C), lambda d, b, s: (blk(d, b, s), 0))
    row_out = pl.BlockSpec((1, CPS, 1, CHUNK, D_C), lambda d, b, s: (d, s, b, 0, 0))
    mat_out = pl.BlockSpec((1, CPS, 1, N_PAIR, HS_C, PAIR), lambda d, b, s: (d, s, b, 0, 0, 0))
    return pl.pallas_call(
        _rwkv_chunk_kernel,
        grid=(2, B, ns // CPS),
        in_specs=[dir_in, dir_in, dir_in, tok_in, tok_in, tok_in],
        out_specs=[row_out, row_out, mat_out, mat_out],
        out_shape=[jax.ShapeDtypeStruct((2, ns, B, CHUNK, D_C), BF16)] * 2
        + [jax.ShapeDtypeStruct((2, ns, B, N_PAIR, HS_C, PAIR), F32)] * 2,
        compiler_params=_cparams(("arbitrary", "arbitrary", "arbitrary")),
        name="rwkv_chunks",
    )(lw, kd, bb, v, kn, r)


def _rwkv_carry_kernel(qh_ref, y0_ref, g_ref, j_ref, y_ref, h_ref):
    s = pl.program_id(0)

    @pl.when(s == 0)
    def _():
        h_ref[...] = jnp.zeros_like(h_ref)

    nb = qh_ref.shape[2]
    ri = lax.broadcasted_iota(jnp.int32, (PAIR, PAIR), 0)
    ci = lax.broadcasted_iota(jnp.int32, (PAIR, PAIR), 1)
    same_head = (ri // HS_C) == (ci // HS_C)

    def unfold(x):
        return jnp.where(same_head, jnp.concatenate([x, x], axis=0), 0.0)

    for d in range(2):
        for b in range(nb):
            for p in range(N_PAIR):
                sl = slice(p * PAIR, (p + 1) * PAIR)
                hh, hl = _split2(h_ref[d, b, p])
                gh, gl = _split2(unfold(g_ref[d, 0, b, p]))
                top = _dot(jnp.concatenate([qh_ref[d, 0, b, :, sl], gh], axis=0), hh)
                y_ref[d, 0, b, :, sl] = (top[:CHUNK] + y0_ref[d, 0, b, :, sl].astype(F32)).astype(y_ref.dtype)
                h_ref[d, b, p] = top[CHUNK:] + (_dot(gh, hl) + _dot(gl, hh)) + unfold(j_ref[d, 0, b, p])


def _rwkv_carry(qh, y0, g, j, st):
    B = st.B
    ns = qh.shape[1]
    rows = pl.BlockSpec((2, 1, B, CHUNK, D_C), lambda s: (0, s, 0, 0, 0))
    mats = pl.BlockSpec((2, 1, B, N_PAIR, HS_C, PAIR), lambda s: (0, s, 0, 0, 0, 0))
    return pl.pallas_call(
        _rwkv_carry_kernel,
        grid=(ns,),
        in_specs=[rows, rows, mats, mats],
        out_specs=rows,
        out_shape=jax.ShapeDtypeStruct((2, ns, B, CHUNK, D_C), BF16),
        scratch_shapes=[pltpu.VMEM((2, B, N_PAIR, PAIR, PAIR), F32)],
        compiler_params=_cparams(("arbitrary",)),
        name="rwkv_carry",
    )(qh, y0, g, j)


def _mix_out_kernel(*refs, n_src, lat_tiles, alpha):
    o_a = _stream_tile(refs[0:n_src], lat_tiles)
    o_f = _stream_tile(refs[n_src:2 * n_src], lat_tiles)
    x = _stream_tile(refs[2 * n_src:3 * n_src], lat_tiles)
    (yf_ref, yb_ref, bonus_ref, rg_ref, gate_ref, mod_ref, gnw_ref, gnb_ref, bd_ref, wpa_ref, wpf_ref, wpc_ref, wo_ref,
     lng_ref, lnb_ref, x1_ref, h2_ref) = refs[3 * n_src:]
    n = TM // CHUNK
    y = jnp.concatenate([yf_ref[0, c, 0].astype(F32) + yb_ref[0, n - 1 - c, 0].astype(F32) for c in range(n)], axis=0)
    bd = bd_ref[...]
    mu = _dot_exact_rhs(y, bd) * (1.0 / HS_C)
    yc = y - mu
    var = _dot_exact_rhs(yc * yc, bd) * (1.0 / HS_C)
    yn = yc * lax.rsqrt(var + GN_EPS) * gnw_ref[...] + gnb_ref[...]
    o_c = ((yn + bonus_ref[...]) * rg_ref[...]).astype(BF16)
    g0 = gate_ref[:, 0:D_MODEL].astype(F32)
    g1 = gate_ref[:, D_MODEL:2 * D_MODEL].astype(F32)
    g2 = gate_ref[:, 2 * D_MODEL:].astype(F32)
    y = g0 * _dot(o_a, wpa_ref[...]) + g1 * _dot(o_f, wpf_ref[...]) + g2 * _dot(o_c, wpc_ref[...])
    mix = _dot(y.astype(BF16), wo_ref[...])
    z = alpha * x + mod_ref[0, 2:3, :] * mix
    x1 = _ln_plain(z) * lng_ref[...] + lnb_ref[...]
    x1_ref[...] = x1
    h2_ref[...] = (_ln_plain(x1) * (1.0 + mod_ref[0, 4:5, :]) + mod_ref[0, 3:4, :]).astype(BF16)


def _mix_out(o_a, o_f, y, bonus, rg, gate, x_all, mod, lp, st, n_tiles, alpha):
    nc_ctx, nc_lat, _, _ = _scan_geometry(st)
    n = TM // CHUNK
    assert nc_ctx % n == 0

    def seq_of(i):
        lat = i < st.lat_tiles
        b = jnp.where(lat, i // st.tpb, (i - st.lat_tiles) // st.tpc)
        j = jnp.where(lat, i % st.tpb, (i - st.lat_tiles) % st.tpc)
        return lat, b, j

    def fwd(i):
        lat, b, j = seq_of(i)
        return (0, jnp.where(lat, nc_ctx // n + j, j), b, 0, 0)

    def bwd(i):
        lat, b, j = seq_of(i)
        return (1, jnp.where(lat, nc_ctx // n + (st.tpb - 1 - j), st.tpc - 1 - j), b, 0, 0)

    row = lambda i: (i, 0)
    full = lambda shape: pl.BlockSpec(shape, lambda i: (0,) * len(shape))
    o_a, o_f, x_all = _pair(o_a), _pair(o_f), _pair(x_all)
    assert len(o_a) == len(o_f) == len(x_all)
    return pl.pallas_call(
        functools.partial(_mix_out_kernel, n_src=len(x_all), lat_tiles=st.lat_tiles, alpha=alpha),
        grid=(n_tiles,),
        in_specs=_stream_specs(o_a, TM, D_A, st.lat_tiles) + _stream_specs(o_f, TM, D_F, st.lat_tiles)
        + _stream_specs(x_all, TM, D_MODEL, st.lat_tiles)
        + [pl.BlockSpec((1, n, 1, CHUNK, D_C), fwd), pl.BlockSpec((1, n, 1, CHUNK, D_C), bwd),
                  pl.BlockSpec((TM, D_C), row), pl.BlockSpec((TM, D_C), row),
                  pl.BlockSpec((TM, N_BRANCH * D_MODEL), row),
                  pl.BlockSpec((1, 6, D_MODEL), lambda i: (st.mod_index(i), 0, 0)),
                  full((1, D_C)), full((1, D_C)), full((D_C, D_C)),
                  full((D_A, D_MODEL)), full((D_F, D_MODEL)), full((D_C, D_MODEL)), full((D_MODEL, D_MODEL)),
                  full((1, D_MODEL)), full((1, D_MODEL))],
        out_specs=[pl.BlockSpec((TM, D_MODEL), row), pl.BlockSpec((TM, D_MODEL), row)],
        out_shape=[jax.ShapeDtypeStruct((n_tiles * TM, D_MODEL), F32),
                   jax.ShapeDtypeStruct((n_tiles * TM, D_MODEL), BF16)],
        compiler_params=_cparams(("arbitrary",)),
        name="mix_out",
    )(*o_a, *o_f, *x_all, y, y, bonus, rg, gate, mod, lp["ln_w"], lp["ln_b"], lp["head_ones"],
      lp["w_pa"], lp["w_pf"], lp["w_pc"], lp["w_o"], lp["ln1_g"], lp["ln1_b"])


def _ffn_down_kernel(u_ref, g_ref, gp_ref, gn_ref, x_ref, mod_ref, cw_ref, cb_ref, wd_ref, lng_ref, lnb_ref,
                     o_ref, *, st, alpha):
    i = pl.program_id(0)
    first, last = st.seq_pos(i)
    g = g_ref[...].astype(F32)
    prev_row = jnp.where(first, 0.0, gp_ref[HALO - 1:HALO, :].astype(F32))
    next_row = jnp.where(last, 0.0, gn_ref[0:1, :].astype(F32))
    prev, nxt = _shift_rows(g, prev_row, next_row)
    cw = cw_ref[...] * 0.5
    gh = cw[0:1, :] * prev + cw[1:2, :] * g + cw[2:3, :] * nxt + cb_ref[...] * 0.5
    act = gh * (1.0 + lax.erf(gh * (2.0 ** 0.5)))
    a = (act * u_ref[...].astype(F32)).astype(BF16)
    z = alpha * x_ref[...] + mod_ref[0, 5:6, :] * _dot(a, wd_ref[...])
    o_ref[...] = _ln_plain(z) * lng_ref[...] + lnb_ref[...]


def _ffn_down(ug, x1, mod, lp, st, n_tiles, alpha):
    row = lambda i: (i, 0)
    full = lambda shape: pl.BlockSpec(shape, lambda i: (0,) * len(shape))
    gp, gn = _halo_specs(D_FF, 1, ug.shape[0])
    return pl.pallas_call(
        functools.partial(_ffn_down_kernel, st=st, alpha=alpha),
        grid=(n_tiles,),
        in_specs=[pl.BlockSpec((TM, D_FF), lambda i: (i, 0)), pl.BlockSpec((TM, D_FF), lambda i: (i, 1)), gp, gn,
                  pl.BlockSpec((TM, D_MODEL), row),
                  pl.BlockSpec((1, 6, D_MODEL), lambda i: (st.mod_index(i), 0, 0)),
                  full((3, D_FF)), full((1, D_FF)), full((D_FF, D_MODEL)), full((1, D_MODEL)), full((1, D_MODEL))],
        out_specs=pl.BlockSpec((TM, D_MODEL), row),
        out_shape=jax.ShapeDtypeStruct((n_tiles * TM, D_MODEL), F32),
        compiler_params=_cparams(("arbitrary",)),
        name="ffn_down",
    )(ug, ug, ug, ug, x1, mod, lp["ffn_conv"], lp["ffn_conv_b"], lp["ffn_down"], lp["ln2_g"], lp["ln2_b"])


def _rope_tables(st):
    T = st.T
    pos = jnp.arange(T)
    rowp = (pos // GRID_W).astype(F32)
    colp = (pos % GRID_W).astype(F32)
    n_freq = DH_A // 4
    inv = ROPE_BASE ** (-jnp.arange(n_freq, dtype=F32) / n_freq)
    ar, ac = rowp[:, None] * inv, colp[:, None] * inv
    cos64 = jnp.concatenate([jnp.cos(ar), jnp.cos(ar), jnp.cos(ac), jnp.cos(ac)], axis=1)
    sin64 = jnp.concatenate([-jnp.sin(ar), jnp.sin(ar), -jnp.sin(ac), jnp.sin(ac)], axis=1)
    cos_q = jnp.concatenate([cos64, cos64], axis=1)
    sin_q = jnp.concatenate([sin64, sin64], axis=1)
    return cos_q, sin_q


def _block_diag2(m):
    z = jnp.zeros_like(m[0])
    return jnp.concatenate([jnp.concatenate([m[0], z], axis=1), jnp.concatenate([z, m[1]], axis=1)], axis=0)


def _hi_lo(w):
    return jnp.stack(_split2(w))


def kernel(x, c, ctx, c_ctx, ada_w, ada_b, w_in, lam_qk, subln_g, tshift_mu, rw_w0, rw_w2, rw_a0, rw_a2, rw_g2, rw_kk, rw_ka, rw_rk, rw_lnw, rw_lnb, w_pa, w_pf, w_pc, w_o, ln1_g, ln1_b, ffn_up, ffn_conv, ffn_conv_b, ffn_down, ln2_g, ln2_b):
    B, T, D = x.shape
    Lc = ctx.shape[1]
    depth = w_in.shape[0]
    st = _Stream(B, T, Lc)
    alpha = (2.0 * depth) ** 0.25

    cos_q, sin_q = _rope_tables(st)
    dft_lat = _dft_tables(T)
    dft_ctx = _dft_tables(Lc)
    kf = jnp.arange(F_GROUP_W, dtype=jnp.int32)
    ang = ((kf[:, None] * kf[None, :]) % F_GROUP_W).astype(F32) * (2.0 * math.pi / F_GROUP_W)
    dft_ch = jnp.concatenate([jnp.cos(ang), jnp.sin(ang)], axis=1).astype(BF16)
    hi = jnp.arange(D_C) // HS_C
    head_ones = (hi[:, None] == hi[None, :]).astype(BF16)
    c_all = jnp.concatenate([c, c_ctx[None], jnp.zeros((8 - B - 1, D), F32)], axis=0)

    w_qkvf, w_rest = _cast_bf16(w_in, (OFF_CS, D_IN - OFF_CS))
    (w_up,), (w_dn,) = _cast_bf16(ffn_up), _cast_bf16(ffn_down)
    (w_pa_b,), (w_pf_b,), (w_pc_b,), (w_o_b,) = _cast_bf16(w_pa), _cast_bf16(w_pf), _cast_bf16(w_pc), _cast_bf16(w_o)

    x_all = (x.reshape(B * T, D), ctx.reshape(B * Lc, D))
    for i in range(depth):
        last = i == depth - 1
        lam_init = 0.8 - 0.6 * math.exp(-0.3 * i)
        lq = lam_qk[i].astype(F32)
        lam = (jnp.exp(jnp.sum(lq[0] * lq[1])) - jnp.exp(jnp.sum(lq[2] * lq[3])) + lam_init).reshape(1, 1)
        lp = {
            "mu_cs": tshift_mu[i][:, :D_CS], "mu_co": tshift_mu[i][:, D_CS:],
            "w0": rw_w0[i], "w2bd": _hi_lo(_block_diag2(rw_w2[i])), "a0": rw_a0[i],
            "a2bd": _hi_lo(_block_diag2(rw_a2[i])),
            "g2": _hi_lo(rw_g2[i]), "k_k": rw_kk[i].reshape(1, D_C), "k_a": rw_ka[i].reshape(1, D_C),
            "r_k": rw_rk[i].reshape(1, D_C), "ln_w": rw_lnw[i].reshape(1, D_C), "ln_b": rw_lnb[i].reshape(1, D_C),
            "head_ones": head_ones,
            "w_pa": w_pa_b[i], "w_pf": w_pf_b[i], "w_pc": w_pc_b[i], "w_o": w_o_b[i], "ln1_g": ln1_g[i].reshape(1, D), "ln1_b": ln1_b[i].reshape(1, D),
            "ffn_conv": ffn_conv[i], "ffn_conv_b": ffn_conv_b[i].reshape(1, D_FF),
            "ffn_down": w_dn[i], "ln2_g": ln2_g[i].reshape(1, D), "ln2_b": ln2_b[i].reshape(1, D),
        }
        n_out = st.lat_tiles if last else st.tiles

        mod = _adaln(c_all, ada_w, ada_b, i).reshape(8, 6, D)
        q, k, v, z = _proj(x_all, w_qkvf, i, ((D_A, 512, "plain", F32), (D_A, 512, "rope", BF16),
                                                    (D_A, 512, "plain", BF16), (D_F, 512, "dft", BF16)),
                           (cos_q, sin_q, dft_ch), mod, st)
        cs, co, gate = _proj(x_all, w_rest, i, ((D_CS, D_CS // 2, "plain", F32), (D_CO, D_CO, "plain", F32),
                                                       (N_BRANCH * D_MODEL, 512, "sigmoid", BF16)), (), mod, st)

        subln = subln_g[i].reshape(1, DV_A)
        o_a = _attention(lam, q, k, v, cos_q, sin_q, subln, st, 1.0 - lam_init, True)
        o_f = _dft2(dft_lat, z, B, T, 0)
        if not last:
            o_a = (o_a, _attention(lam, q, k, v, cos_q, sin_q, subln, st, 1.0 - lam_init, False))
            o_f = (o_f, _dft2(dft_ctx, z, B, Lc, st.n_lat_rows))
        x_mix = x_all
        if last and isinstance(x_all, tuple):
            x_mix = x_all[0]

        lw, kd, bb, rv, kn, rr, bonus, rg = _rwkv_prep(cs, co, lp, st)
        qh, y0, gm, jm = _rwkv_chunks(lw, kd, bb, rv, kn, rr, st)
        y = _rwkv_carry(qh, y0, gm, jm, st)
        x1, h2 = _mix_out(o_a, o_f, y, bonus, rg, gate, x_mix, mod, lp, st, n_out, alpha)
        (ug,) = _proj(h2, w_up, i, ((2 * D_FF, 512, "plain", BF16),))
        x_all = _ffn_down(ug, x1, mod, lp, st, n_out, alpha)
    return x_all[:B * T].reshape(B, T, D)
```

```python
import functools
import math

import jax
import jax.numpy as jnp
from jax import lax
from jax.experimental import pallas as pl
from jax.experimental.pallas import tpu as pltpu

F32 = jnp.float32
BF16 = jnp.bfloat16

D_MODEL = 1024
DEPTH = 2
GRID_W = 64
HA = 8
DH_A = 64
DV_A = 2 * DH_A
D_A = HA * DV_A
ROPE_BASE = 10000.0
F_GROUPS = 4
F_GROUP_W = 128
D_F = F_GROUPS * F_GROUP_W
HC = 8
HS_C = 64
D_C = HC * HS_C
LORA_W = 64
LORA_A = 64
LORA_G = 128
D_CS = 2 * D_C + 2 * LORA_W + 2 * LORA_A
D_CO = D_C + LORA_G
N_BRANCH = 3
D_FF = 2816
LN_EPS = 1e-5
GN_EPS = HS_C * 1e-5
HEAD_NORM_EPS = 1e-5

OFF_Q = 0
OFF_K = OFF_Q + D_A
OFF_V = OFF_K + D_A
OFF_F = OFF_V + D_A
OFF_CS = OFF_F + D_F
OFF_CO = OFF_CS + D_CS
OFF_GATE = OFF_CO + D_CO
D_IN = OFF_GATE + N_BRANCH * D_MODEL

LANES = 128
HALO = 16
TM = 256
TMP = 512
FF_COLS = 256
CHUNK = 64
PAIR = 2 * HS_C
CPS = 4
N_PAIR = D_C // PAIR
VMEM_LIMIT = 56 * 1024 * 1024


def _cparams(sem):
    return pltpu.CompilerParams(dimension_semantics=sem, vmem_limit_bytes=VMEM_LIMIT)


def _split2(x):
    hi = x.astype(BF16)
    lo = (x - hi.astype(F32)).astype(BF16)
    return hi, lo


def _dot(a, b, dims=(((1,), (0,)), ((), ()))):
    return lax.dot_general(a, b, dims, preferred_element_type=F32)


def _dot_nt(a, b):
    return lax.dot_general(a, b, (((1,), (1,)), ((), ())), preferred_element_type=F32)


def _dot_tn(a, b):
    return lax.dot_general(a, b, (((0,), (0,)), ((), ())), preferred_element_type=F32)


def _dotb(a, b):
    return _dot(a.astype(BF16), b.astype(BF16))


def _dot3(a, b):
    ah, al = _split2(a)
    bh, bl = _split2(b)
    return _dot(ah, bh) + (_dot(ah, bl) + _dot(al, bh))


def _dot3w(a, w_ref):
    ah, al = _split2(a)
    return _dot(ah, w_ref[0]) + (_dot(ah, w_ref[1]) + _dot(al, w_ref[0]))


def _dot_exact_rhs(a, b_bf16):
    a1, a2 = _split2(a)
    return _dot(a1, b_bf16) + _dot(a2, b_bf16)


def _ln_plain(x):
    mu = jnp.mean(x, axis=-1, keepdims=True)
    xc = x - mu
    var = jnp.mean(xc * xc, axis=-1, keepdims=True)
    return xc * lax.rsqrt(var + LN_EPS)


def _sigmoid(x):
    return 1.0 / (1.0 + jnp.exp(-x))


def _swap16(x):
    n = x.shape[-1]
    lane = lax.broadcasted_iota(jnp.int32, x.shape, x.ndim - 1)
    up = pltpu.roll(x, n - 16, axis=x.ndim - 1)
    dn = pltpu.roll(x, 16, axis=x.ndim - 1)
    return jnp.where((lane % 32) < 16, up, dn)


def _shift_rows(x, prev_row, next_row):
    n = x.shape[0]
    row = lax.broadcasted_iota(jnp.int32, x.shape, 0)
    prev = jnp.where(row == 0, prev_row, pltpu.roll(x, 1, axis=0))
    nxt = jnp.where(row == n - 1, next_row, pltpu.roll(x, n - 1, axis=0))
    return prev, nxt


class _Stream:
    def __init__(self, B, T, Lc):
        assert T % TM == 0 and Lc % TM == 0 and (B * T) % Lc == 0
        self.B, self.T, self.Lc = B, T, Lc
        self.n_lat_rows = B * T
        self.n_rows = B * T + B * Lc
        self.lat_tiles = B * T // TM
        self.tiles = self.n_rows // TM
        self.tpb = T // TM
        self.tpc = Lc // TM

    def mod_index(self, i):
        return jnp.where(i < self.lat_tiles, i // self.tpb, self.B)

    def seq_pos(self, i):
        j = jnp.where(i < self.lat_tiles, i % self.tpb, (i - self.lat_tiles) % self.tpc)
        n = jnp.where(i < self.lat_tiles, self.tpb, self.tpc)
        return j == 0, j == n - 1


def _pair(a):
    return a if isinstance(a, tuple) else (a,)


def _stream_specs(parts, tile, width, n_lat_tiles):
    if len(parts) == 1:
        return [pl.BlockSpec((tile, width), lambda i: (i, 0))]
    return [pl.BlockSpec((tile, width), lambda i: (jnp.minimum(i, n_lat_tiles - 1), 0)),
            pl.BlockSpec((tile, width), lambda i: (jnp.maximum(i - n_lat_tiles, 0), 0))]


def _stream_tile(refs, n_lat_tiles):
    if len(refs) == 1:
        return refs[0][...]
    return jnp.where(pl.program_id(0) < n_lat_tiles, refs[0][...], refs[1][...])


def _halo_specs(width, col_block, n_rows):
    per = TM // HALO
    last = n_rows // HALO - 1
    prev = pl.BlockSpec((HALO, width), lambda i: (jnp.maximum(i * per - 1, 0), col_block))
    nxt = pl.BlockSpec((HALO, width), lambda i: (jnp.minimum((i + 1) * per, last), col_block))
    return prev, nxt


def _adaln_kernel(c_ref, w_ref, b_ref, o_ref):
    c = c_ref[...]
    s = c * _sigmoid(c)
    o_ref[...] = _dot3(s, w_ref[...]) + b_ref[...]


def _adaln(c_all, w, b, layer):
    n = w.shape[2]
    tn = 1536
    return pl.pallas_call(
        _adaln_kernel,
        grid=(n // tn,),
        in_specs=[pl.BlockSpec((8, D_MODEL), lambda j: (0, 0)),
                  pl.BlockSpec((None, D_MODEL, tn), lambda j: (layer, 0, j)),
                  pl.BlockSpec((None, 1, tn), lambda j: (layer, 0, j))],
        out_specs=pl.BlockSpec((8, tn), lambda j: (0, j)),
        out_shape=jax.ShapeDtypeStruct((8, n), F32),
        compiler_params=_cparams(("arbitrary",)),
        name="adaln",
    )(c_all, w, b.reshape(b.shape[0], 1, n))


def _cast_kernel(w_ref, *o_refs):
    off = 0
    for o_ref in o_refs:
        width = o_ref.shape[-1]
        o_ref[...] = w_ref[:, :, off:off + width].astype(o_ref.dtype)
        off += width


def _cast_bf16(w, splits=None):
    n_l, rows, cols = w.shape
    splits = splits or (cols,)
    assert sum(splits) == cols and rows % TM == 0
    return pl.pallas_call(
        _cast_kernel,
        grid=(n_l, rows // TM),
        in_specs=[pl.BlockSpec((1, TM, cols), lambda l, r: (l, r, 0))],
        out_specs=[pl.BlockSpec((1, TM, c), lambda l, r: (l, r, 0)) for c in splits],
        out_shape=[jax.ShapeDtypeStruct((n_l, rows, c), BF16) for c in splits],
        compiler_params=_cparams(("arbitrary", "arbitrary")),
        name="cast_weights",
    )(w)


def _proj_kernel(*refs, n_src, groups, has_mod, n_tables, lat_tiles):
    a = _stream_tile(refs[:n_src], lat_tiles)
    w_ref, rest = refs[n_src], refs[n_src + 1:]
    if has_mod:
        mod_ref, rest = rest[0], rest[1:]
        a = (_ln_plain(a) * (1.0 + mod_ref[0, 1:2, :]) + mod_ref[0, 0:1, :]).astype(BF16)
    tables, outs = rest[:n_tables], rest[n_tables:]
    off = 0
    for (width, chunk, epi), o_ref in zip(groups, outs):
        for c0 in range(0, width, chunk):
            acc = _dot(a, w_ref[:, off + c0:off + c0 + chunk])
            if epi == "rope":
                reps = chunk // LANES
                lat = pl.program_id(0) < lat_tiles
                cos = jnp.where(lat, jnp.concatenate([tables[0][...]] * reps, axis=1), 1.0)
                sin = jnp.where(lat, jnp.concatenate([tables[1][...]] * reps, axis=1), 0.0)
                acc = acc * cos + _swap16(acc) * sin
            elif epi == "sigmoid":
                acc = _sigmoid(acc)
            if epi == "dft":
                fb = acc.astype(BF16)
                for g in range(chunk // F_GROUP_W):
                    sl = slice(c0 + g * F_GROUP_W, c0 + (g + 1) * F_GROUP_W)
                    r = _dot(fb[:, g * F_GROUP_W:(g + 1) * F_GROUP_W], tables[2][...])
                    o_ref[0, :, sl] = r[:, :F_GROUP_W].astype(o_ref.dtype)
                    o_ref[1, :, sl] = r[:, F_GROUP_W:].astype(o_ref.dtype)
            else:
                o_ref[:, c0:c0 + chunk] = acc.astype(o_ref.dtype)
        off += width


def _proj(h, w, layer, groups, tables=(), mod=None, st=None):
    srcs = _pair(h)
    m, k = sum(a.shape[0] for a in srcs), srcs[0].shape[1]
    n = w.shape[2]
    assert all(a.shape[0] % TMP == 0 for a in srcs) and sum(g[0] for g in groups) == n
    assert all(g[0] % g[1] == 0 for g in groups)
    row = lambda i: (i, 0)
    lat_tiles, per_seq = (st.n_lat_rows // TMP, st.T // TMP) if st is not None else (0, 1)
    specs = _stream_specs(srcs, TMP, k, lat_tiles) + [pl.BlockSpec((None, k, n), lambda i: (layer, 0, 0))]
    args = list(srcs) + [w]
    if mod is not None:
        specs.append(pl.BlockSpec((1, 6, D_MODEL), lambda i: (jnp.where(i < lat_tiles, i // per_seq, st.B), 0, 0)))
        args.append(mod)
    for t in tables:
        if st is not None and t.shape == (st.T, LANES):
            specs.append(pl.BlockSpec((TMP, LANES), lambda i: (jnp.where(i < lat_tiles, i % per_seq, 0), 0)))
        else:
            specs.append(pl.BlockSpec(t.shape, lambda i: (0, 0)))
    out_specs = [pl.BlockSpec((2, TMP, g[0]), lambda i: (0, i, 0)) if g[2] == "dft" else pl.BlockSpec((TMP, g[0]), row)
                 for g in groups]
    out_shape = [jax.ShapeDtypeStruct((2, m, g[0]) if g[2] == "dft" else (m, g[0]), g[3]) for g in groups]
    return pl.pallas_call(
        functools.partial(_proj_kernel, n_src=len(srcs), groups=tuple(g[:3] for g in groups), has_mod=mod is not None,
                          n_tables=len(tables), lat_tiles=lat_tiles),
        grid=(m // TMP,),
        in_specs=specs,
        out_specs=out_specs,
        out_shape=out_shape,
        compiler_params=_cparams(("arbitrary",)),
        name="proj_" + "_".join(g[2] for g in groups),
    )(*args, *tables)


TQ = 512
TK = 512
AHEAD = 1


def _attn_kernel(lam_ref, q_ref, cos_ref, sin_ref, g_ref, kc_ref, vc_ref, *rest, n_kblk, out_scale):
    if n_kblk:
        kl_ref, vl_ref, o_ref = rest
    else:
        (o_ref,) = rest
    tq = q_ref.shape[0]
    lane = lax.broadcasted_iota(jnp.int32, (tq, LANES), 1)
    first = lane < DH_A
    q = q_ref[...] * (DH_A ** -0.5 * math.log2(math.e))

    def stack(x):
        return jnp.concatenate([jnp.where(first, x, 0.0), jnp.where(first, 0.0, x)], axis=0).astype(BF16)

    def update(s, v, m, acc):
        m_new = jnp.maximum(m, jnp.max(s, axis=-1, keepdims=True))
        p = jnp.exp2((s - m_new).astype(BF16))
        pv = _dot(p, jnp.concatenate([v, jnp.ones_like(v)], axis=1))
        return m_new, acc * jnp.exp2(m - m_new) + pv

    carry = (jnp.full((2 * tq, 1), -1e30, F32), jnp.zeros((2 * tq, 2 * DV_A), F32))
    pending = [(_dot_nt(stack(q), kc_ref[...]), vc_ref[...])]
    if n_kblk:
        q_rot = stack(q * cos_ref[...] + _swap16(q) * sin_ref[...])
        for j in range(n_kblk):
            pending.append((_dot_nt(q_rot, kl_ref[j * TK:(j + 1) * TK, :]), vl_ref[j * TK:(j + 1) * TK, :]))
            if len(pending) > AHEAD:
                carry = update(*pending.pop(0), *carry)
    for s, v in pending:
        carry = update(s, v, *carry)
    _, acc = carry
    o = acc[:, :DV_A] / acc[:, DV_A:DV_A + 1]
    o = o[:tq] - lam_ref[0, 0] * o[tq:]
    o = o * lax.rsqrt(jnp.mean(o * o, axis=-1, keepdims=True) + HEAD_NORM_EPS)
    o_ref[...] = (o * (g_ref[...] * out_scale)).astype(o_ref.dtype)


def _attention(lam, q, k, v, cos_q, sin_q, subln, st, out_scale, latent):
    B, T, Lc = st.B, st.T, st.Lc
    ctx_blk0 = B * T // Lc
    tq = min(TQ, T if latent else Lc)
    n_q = (T if latent else Lc) // tq
    q_row0 = 0 if latent else B * T // tq
    n_kblk = T // TK if latent else 0
    in_specs = [
        pl.BlockSpec(memory_space=pltpu.SMEM),
        pl.BlockSpec((tq, LANES), lambda b, h, i: (q_row0 + b * n_q + i, h)),
        pl.BlockSpec((tq, LANES), lambda b, h, i: (i if latent else 0, 0)),
        pl.BlockSpec((tq, LANES), lambda b, h, i: (i if latent else 0, 0)),
        pl.BlockSpec((1, LANES), lambda b, h, i: (0, 0)),
        pl.BlockSpec((Lc, LANES), lambda b, h, i: (ctx_blk0 + b, h)),
        pl.BlockSpec((Lc, LANES), lambda b, h, i: (ctx_blk0 + b, h)),
    ]
    args = [lam, q, cos_q, sin_q, subln, k, v]
    if latent:
        in_specs += [pl.BlockSpec((T, LANES), lambda b, h, i: (b, h)),
                     pl.BlockSpec((T, LANES), lambda b, h, i: (b, h))]
        args += [k, v]
    return pl.pallas_call(
        functools.partial(_attn_kernel, n_kblk=n_kblk, out_scale=out_scale),
        grid=(B, HA, n_q),
        in_specs=in_specs,
        out_specs=pl.BlockSpec((tq, LANES), lambda b, h, i: (b * n_q + i, h)),
        out_shape=jax.ShapeDtypeStruct((B * n_q * tq, D_A), BF16),
        compiler_params=_cparams(("arbitrary", "arbitrary", "arbitrary")),
        name="attn_lat" if latent else "attn_ctx",
    )(*args)


def _dft2_kernel(a_ref, z_ref, o_ref, acc_ref, *, nk, scale):
    k = pl.program_id(2)

    @pl.when(k == 0)
    def _():
        acc_ref[...] = jnp.zeros_like(acc_ref)

    acc_ref[...] += _dot(a_ref[...], z_ref[0])

    @pl.when(k == nk - 1)
    def _():
        o_ref[...] = (acc_ref[...] * scale).astype(o_ref.dtype)


def _dft2(a_mat, z, n_seq, t_len, row0):
    tm = min(1024, t_len)
    tk = min(2048, t_len)
    kb = t_len // tk
    nk = 2 * kb
    blk0 = row0 // tk
    scale = 1.0 / math.sqrt(t_len * F_GROUP_W)
    return pl.pallas_call(
        functools.partial(_dft2_kernel, nk=nk, scale=scale),
        grid=(n_seq, t_len // tm, nk),
        in_specs=[pl.BlockSpec((tm, tk), lambda b, i, k: (i, k)),
                  pl.BlockSpec((1, tk, D_F), lambda b, i, k: (k // kb, blk0 + b * kb + k % kb, 0))],
        out_specs=pl.BlockSpec((tm, D_F), lambda b, i, k: (b * (t_len // tm) + i, 0)),
        out_shape=jax.ShapeDtypeStruct((n_seq * t_len, D_F), BF16),
        scratch_shapes=[pltpu.VMEM((tm, D_F), F32)],
        compiler_params=_cparams(("arbitrary", "arbitrary", "arbitrary")),
        name="dft_positions",
    )(a_mat, z)


def _dft_tables(t_len):
    n_hi = t_len // LANES
    k = jnp.arange(t_len, dtype=jnp.int32)[:, None]
    unit = 2.0 * math.pi / t_len
    a_hi = ((k * (jnp.arange(n_hi, dtype=jnp.int32) * LANES)[None, :]) % t_len).astype(F32) * unit
    a_lo = ((k * jnp.arange(LANES, dtype=jnp.int32)[None, :]) % t_len).astype(F32) * unit

    def table_kernel(c1_ref, s1_ref, c2_ref, s2_ref, o_ref):
        c2, s2 = c2_ref[...], s2_ref[...]
        for hi in range(n_hi):
            c1, s1 = c1_ref[:, hi:hi + 1], s1_ref[:, hi:hi + 1]
            o_ref[:, hi * LANES:(hi + 1) * LANES] = (c1 * c2 - s1 * s2).astype(BF16)
            o_ref[:, t_len + hi * LANES:t_len + (hi + 1) * LANES] = (-(s1 * c2 + c1 * s2)).astype(BF16)

    row = lambda i: (i, 0)
    return pl.pallas_call(
        table_kernel,
        grid=(t_len // TM,),
        in_specs=[pl.BlockSpec((TM, n_hi), row), pl.BlockSpec((TM, n_hi), row),
                  pl.BlockSpec((TM, LANES), row), pl.BlockSpec((TM, LANES), row)],
        out_specs=pl.BlockSpec((TM, 2 * t_len), row),
        out_shape=jax.ShapeDtypeStruct((t_len, 2 * t_len), BF16),
        compiler_params=_cparams(("arbitrary",)),
        name="dft_table",
    )(jnp.cos(a_hi), jnp.sin(a_hi), jnp.cos(a_lo), jnp.sin(a_lo))


def _rwkv_prep_kernel(cs_ref, csp_ref, csn_ref, co_ref, cop_ref, con_ref, mucs_ref, muco_ref,
                      w0_ref, w2_ref, a0_ref, a2_ref, g2_ref, kk_ref, ka_ref, rk_ref, bd_ref,
                      lw_ref, kd_ref, bb_ref, v_ref, kn_ref, r_ref, bonus_ref, g_ref, *, st):
    i = pl.program_id(0)
    first, last = st.seq_pos(i)

    def tshift(x_ref, p_ref, n_ref, mu_ref):
        x = x_ref[...]
        prev_row = jnp.where(first, 0.0, p_ref[HALO - 1:HALO, :])
        next_row = jnp.where(last, 0.0, n_ref[0:1, :])
        prev, nxt = _shift_rows(x, prev_row, next_row)
        return x + mu_ref[0:1, :] * (prev - x) + mu_ref[1:2, :] * (nxt - x)

    cs = tshift(cs_ref, csp_ref, csn_ref, mucs_ref)
    co = tshift(co_ref, cop_ref, con_ref, muco_ref)
    k = cs[:, :D_C]
    v = cs[:, D_C:2 * D_C]
    wd = cs[:, 2 * D_C:2 * D_C + 2 * LORA_W]
    ad = cs[:, 2 * D_C + 2 * LORA_W:]
    r = co[:, :D_C]
    gd = co[:, D_C:]
    bd = bd_ref[...]

    zw = _dot3w(jnp.tanh(wd), w2_ref)
    za = _dot3w(ad, a2_ref)
    kx = k * kk_ref[...]
    ss = _dot_exact_rhs(kx * kx, bd)
    kn = kx / jnp.maximum(jnp.sqrt(ss), 1e-12)
    kd_sum = jnp.zeros_like(k)
    for d in range(2):
        sl = slice(d * D_C, (d + 1) * D_C)
        z = -(w0_ref[d:d + 1, :] + zw[:, sl])
        softplus = jnp.maximum(z, 0.0) + jnp.log(1.0 + jnp.exp(-jnp.abs(z)))
        lw_ref[d] = -jnp.exp(-softplus - 0.5)
        a = _sigmoid(a0_ref[d:d + 1, :] + za[:, sl])
        kd = k * (1.0 + (a - 1.0) * ka_ref[...])
        kd_ref[d] = kd
        bb_ref[d] = kn * a
        kd_sum = kd_sum + kd
    v_ref[...] = v
    kn_ref[...] = kn
    r_ref[...] = r
    bonus_ref[...] = _dot_exact_rhs(r * kd_sum * rk_ref[...], bd) * v
    g_ref[...] = _dot3w(_sigmoid(gd), g2_ref)


def _rwkv_prep(cs, co, lp, st):
    n = st.n_rows
    row = lambda i: (i, 0)
    full = lambda shape: pl.BlockSpec(shape, lambda i: (0,) * len(shape))
    csp, csn = _halo_specs(D_CS, 0, n)
    cop, con = _halo_specs(D_CO, 0, n)
    dir_out = pl.BlockSpec((2, TM, D_C), lambda i: (0, i, 0))
    tok_out = pl.BlockSpec((TM, D_C), row)
    return pl.pallas_call(
        functools.partial(_rwkv_prep_kernel, st=st),
        grid=(st.tiles,),
        in_specs=[pl.BlockSpec((TM, D_CS), row), csp, csn, pl.BlockSpec((TM, D_CO), row), cop, con,
                  full((2, D_CS)), full((2, D_CO)), full((2, D_C)), full((2, 2 * LORA_W, 2 * D_C)),
                  full((2, D_C)), full((2, 2 * LORA_A, 2 * D_C)), full((2, LORA_G, D_C)),
                  full((1, D_C)), full((1, D_C)), full((1, D_C)), full((D_C, D_C))],
        out_specs=[dir_out, dir_out, dir_out, tok_out, tok_out, tok_out, tok_out, tok_out],
        out_shape=[jax.ShapeDtypeStruct((2, n, D_C), F32)] * 3 + [jax.ShapeDtypeStruct((n, D_C), F32)] * 5,
        compiler_params=_cparams(("arbitrary",)),
        name="rwkv_prep",
    )(cs, cs, cs, co, co, co, lp["mu_cs"], lp["mu_co"], lp["w0"], lp["w2bd"], lp["a0"], lp["a2bd"],
      lp["g2"], lp["k_k"], lp["k_a"], lp["r_k"], lp["head_ones"])


def _rwkv_chunk_kernel(lw_ref, kd_ref, bb_ref, v_ref, kn_ref, r_ref, qh_ref, y0_ref, g_ref, j_ref):
    d = pl.program_id(0)
    L = CHUNK
    sgn = jnp.where(d == 0, 1, -1)
    row_t = lax.broadcasted_iota(jnp.int32, (L, D_C), 0)
    ri = lax.broadcasted_iota(jnp.int32, (PAIR, PAIR), 0)
    ci = lax.broadcasted_iota(jnp.int32, (PAIR, PAIR), 1)
    same_head = (ri // HS_C) == (ci // HS_C)
    lag = jnp.where(same_head, (ri % L - ci % L) * sgn, -1)
    m_strict = lag > 0
    m_incl = lag >= 0
    eye = (ri == ci).astype(F32)
    lane = lax.broadcasted_iota(jnp.int32, (L, D_C), 1)
    head0 = (lane % PAIR) < HS_C

    def stack(x):
        return jnp.concatenate([jnp.where(head0, x, 0.0), jnp.where(head0, 0.0, x)], axis=0)

    def fold(x):
        return x[:L] + x[L:]

    pairs = range(N_PAIR)
    sls = [slice(p * PAIR, (p + 1) * PAIR) for p in pairs]
    H = PAIR
    units = [(c, p) for c in range(CPS) for p in pairs]

    def wide(c):
        rs = slice(c * L, (c + 1) * L)
        lw, kd, bb = lw_ref[0, rs, :], kd_ref[0, rs, :], bb_ref[0, rs, :]
        v, kn, r = v_ref[rs, :], kn_ref[rs, :], r_ref[rs, :]
        tot = jnp.sum(lw, axis=0, keepdims=True)
        pre = lw
        span = 1
        while span < L:
            pre = pre + jnp.where(row_t >= span, pltpu.roll(pre, span, axis=0), 0.0)
            span *= 2
        cum = jnp.where(d == 0, pre, tot - pre + lw)
        e_inv = jnp.exp(-cum)
        e_end = jnp.exp(tot - cum)
        xr_f = stack(r * jnp.exp(cum))
        a_t = -bb * e_inv
        k_t = kd * e_inv
        return dict(
            e_tot=jnp.exp(tot), xr_f=xr_f,
            rows=jnp.concatenate([stack(kn * jnp.exp(cum - lw)), xr_f], axis=0).astype(BF16),
            cols=jnp.concatenate([a_t, a_t, k_t, k_t], axis=0).astype(BF16),
            vs=stack(v).astype(BF16),
            ends=jnp.concatenate([-bb * e_end, kd * e_end], axis=0).astype(BF16),
            v_b=v.astype(BF16))

    ch = [wide(c) for c in range(CPS)]
    big = [_dot_nt(ch[c]["rows"][:, sls[p]], ch[c]["cols"][:, sls[p]]) for c, p in units]
    n_mat = [jnp.where(m_strict, x[:H, :H], 0.0) for x in big]
    m_bk = [jnp.where(m_strict, x[:H, H:], 0.0).astype(BF16) for x in big]
    m_ra = [jnp.where(m_incl, x[H:, :H], 0.0).astype(BF16) for x in big]
    m_rk = [jnp.where(m_incl, x[H:, H:], 0.0).astype(BF16) for x in big]
    n_u = range(len(units))
    u1 = [_dot(m_bk[u], ch[c]["vs"][:, sls[p]]) for u, (c, p) in enumerate(units)]
    t_mat = [eye + x for x in n_mat]
    n_b = [x.astype(BF16) for x in n_mat]
    n_pow = [_dot(x, x).astype(BF16) for x in n_b]
    span = 2
    while span < L // 2:
        res = [_dot(jnp.concatenate([n_pow[u], t_mat[u].astype(BF16)], axis=0), n_pow[u]) for u in n_u]
        n_pow = [x[:H].astype(BF16) for x in res]
        t_mat = [t_mat[u] + res[u][H:] for u in n_u]
        span *= 2
    t_mat = [t_mat[u] + _dot(t_mat[u].astype(BF16), n_pow[u]) for u in n_u]
    wu = [_dot(t_mat[u].astype(BF16), jnp.concatenate([ch[c]["rows"][:H, sls[p]], u1[u].astype(BF16)], axis=1))
          for u, (c, p) in enumerate(units)]
    qy = [_dot(m_ra[u], wu[u].astype(BF16)) for u in n_u]
    y0x = [_dot(m_rk[u], ch[c]["vs"][:, sls[p]]) for u, (c, p) in enumerate(units)]
    zeros = jnp.zeros((L, H), BF16)
    gj = [_dot_tn(ch[c]["ends"][:, sls[p]],
                  jnp.concatenate([fold(wu[u]).astype(BF16),
                                   jnp.concatenate([zeros, ch[c]["v_b"][:, sls[p]]], axis=1)], axis=0))
          for u, (c, p) in enumerate(units)]
    for u, (c, p) in enumerate(units):
        pos = jnp.where(d == 0, c, CPS - 1 - c)
        qh_ref[0, pos, 0, :, sls[p]] = fold(ch[c]["xr_f"][:, sls[p]] + qy[u][:, :H]).astype(qh_ref.dtype)
        y0_ref[0, pos, 0, :, sls[p]] = fold(qy[u][:, H:] + y0x[u]).astype(y0_ref.dtype)
        g_ref[0, pos, 0, p] = fold(jnp.where(same_head, gj[u][:, :H], 0.0) + eye * ch[c]["e_tot"][:, sls[p]])
        j_ref[0, pos, 0, p] = fold(jnp.where(same_head, gj[u][:, H:], 0.0))


def _scan_geometry(st):
    nc_ctx = st.Lc // CHUNK
    nc_lat = st.T // CHUNK
    ns = nc_ctx + nc_lat
    ctx_base = st.n_lat_rows // CHUNK

    def row_block(d, b, s):
        in_ctx = s < nc_ctx
        j_ctx = jnp.where(d == 0, s, nc_ctx - 1 - s)
        j_lat = jnp.where(d == 0, s - nc_ctx, nc_lat - 1 - (s - nc_ctx))
        return jnp.where(in_ctx, ctx_base + b * nc_ctx + j_ctx, b * nc_lat + j_lat)

    return nc_ctx, nc_lat, ns, row_block


def _rwkv_chunks(lw, kd, bb, v, kn, r, st):
    B = st.B
    nc_ctx, nc_lat, ns, row_block = _scan_geometry(st)
    assert nc_ctx % CPS == 0 and nc_lat % CPS == 0
    rows = CPS * CHUNK
    blk = lambda d, b, s: row_block(d, b, s * CPS) // CPS
    dir_in = pl.BlockSpec((1, rows, D_C), lambda d, b, s: (d, blk(d, b, s), 0))
    tok_in = pl.BlockSpec((rows, D_C), lambda d, b, s: (blk(d, b, s), 0))
    row_out = pl.BlockSpec((1, CPS, 1, CHUNK, D_C), lambda d, b, s: (d, s, b, 0, 0))
    mat_out = pl.BlockSpec((1, CPS, 1, N_PAIR, HS_C, PAIR), lambda d, b, s: (d, s, b, 0, 0, 0))
    return pl.pallas_call(
        _rwkv_chunk_kernel,
        grid=(2, B, ns // CPS),
        in_specs=[dir_in, dir_in, dir_in, tok_in, tok_in, tok_in],
        out_specs=[row_out, row_out, mat_out, mat_out],
        out_shape=[jax.ShapeDtypeStruct((2, ns, B, CHUNK, D_C), BF16)] * 2
        + [jax.ShapeDtypeStruct((2, ns, B, N_PAIR, HS_C, PAIR), F32)] * 2,
        compiler_params=_cparams(("arbitrary", "arbitrary", "arbitrary")),
        name="rwkv_chunks",
    )(lw, kd, bb, v, kn, r)


def _rwkv_carry_kernel(qh_ref, y0_ref, g_ref, j_ref, y_ref, h_ref):
    s = pl.program_id(0)

    @pl.when(s == 0)
    def _():
        h_ref[...] = jnp.zeros_like(h_ref)

    nb = qh_ref.shape[2]
    ri = lax.broadcasted_iota(jnp.int32, (PAIR, PAIR), 0)
    ci = lax.broadcasted_iota(jnp.int32, (PAIR, PAIR), 1)
    same_head = (ri // HS_C) == (ci // HS_C)

    def unfold(x):
        return jnp.where(same_head, jnp.concatenate([x, x], axis=0), 0.0)

    for d in range(2):
        for b in range(nb):
            for p in range(N_PAIR):
                sl = slice(p * PAIR, (p + 1) * PAIR)
                hh, hl = _split2(h_ref[d, b, p])
                gh, gl = _split2(unfold(g_ref[d, 0, b, p]))
                top = _dot(jnp.concatenate([qh_ref[d, 0, b, :, sl], gh], axis=0), hh)
                y_ref[d, 0, b, :, sl] = (top[:CHUNK] + y0_ref[d, 0, b, :, sl].astype(F32)).astype(y_ref.dtype)
                h_ref[d, b, p] = top[CHUNK:] + (_dot(gh, hl) + _dot(gl, hh)) + unfold(j_ref[d, 0, b, p])


def _rwkv_carry(qh, y0, g, j, st):
    B = st.B
    ns = qh.shape[1]
    rows = pl.BlockSpec((2, 1, B, CHUNK, D_C), lambda s: (0, s, 0, 0, 0))
    mats = pl.BlockSpec((2, 1, B, N_PAIR, HS_C, PAIR), lambda s: (0, s, 0, 0, 0, 0))
    return pl.pallas_call(
        _rwkv_carry_kernel,
        grid=(ns,),
        in_specs=[rows, rows, mats, mats],
        out_specs=rows,
        out_shape=jax.ShapeDtypeStruct((2, ns, B, CHUNK, D_C), BF16),
        scratch_shapes=[pltpu.VMEM((2, B, N_PAIR, PAIR, PAIR), F32)],
        compiler_params=_cparams(("arbitrary",)),
        name="rwkv_carry",
    )(qh, y0, g, j)


def _mix_out_kernel(*refs, n_src, lat_tiles, alpha):
    o_a = _stream_tile(refs[0:n_src], lat_tiles)
    o_f = _stream_tile(refs[n_src:2 * n_src], lat_tiles)
    x = _stream_tile(refs[2 * n_src:3 * n_src], lat_tiles)
    (yf_ref, yb_ref, bonus_ref, rg_ref, gate_ref, mod_ref, gnw_ref, gnb_ref, bd_ref, wpa_ref, wpf_ref, wpc_ref, wo_ref,
     lng_ref, lnb_ref, x1_ref, h2_ref) = refs[3 * n_src:]
    n = TM // CHUNK
    y = jnp.concatenate([yf_ref[0, c, 0].astype(F32) + yb_ref[0, n - 1 - c, 0].astype(F32) for c in range(n)], axis=0)
    bd = bd_ref[...]
    mu = _dot_exact_rhs(y, bd) * (1.0 / HS_C)
    yc = y - mu
    var = _dot_exact_rhs(yc * yc, bd) * (1.0 / HS_C)
    yn = yc * lax.rsqrt(var + GN_EPS) * gnw_ref[...] + gnb_ref[...]
    o_c = ((yn + bonus_ref[...]) * rg_ref[...]).astype(BF16)
    g0 = gate_ref[:, 0:D_MODEL].astype(F32)
    g1 = gate_ref[:, D_MODEL:2 * D_MODEL].astype(F32)
    g2 = gate_ref[:, 2 * D_MODEL:].astype(F32)
    y = g0 * _dot(o_a, wpa_ref[...]) + g1 * _dot(o_f, wpf_ref[...]) + g2 * _dot(o_c, wpc_ref[...])
    mix = _dot(y.astype(BF16), wo_ref[...])
    z = alpha * x + mod_ref[0, 2:3, :] * mix
    x1 = _ln_plain(z) * lng_ref[...] + lnb_ref[...]
    x1_ref[...] = x1
    h2_ref[...] = (_ln_plain(x1) * (1.0 + mod_ref[0, 4:5, :]) + mod_ref[0, 3:4, :]).astype(BF16)


def _mix_out(o_a, o_f, y, bonus, rg, gate, x_all, mod, lp, st, n_tiles, alpha):
    nc_ctx, nc_lat, _, _ = _scan_geometry(st)
    n = TM // CHUNK
    assert nc_ctx % n == 0

    def seq_of(i):
        lat = i < st.lat_tiles
        b = jnp.where(lat, i // st.tpb, (i - st.lat_tiles) // st.tpc)
        j = jnp.where(lat, i % st.tpb, (i - st.lat_tiles) % st.tpc)
        return lat, b, j

    def fwd(i):
        lat, b, j = seq_of(i)
        return (0, jnp.where(lat, nc_ctx // n + j, j), b, 0, 0)

    def bwd(i):
        lat, b, j = seq_of(i)
        return (1, jnp.where(lat, nc_ctx // n + (st.tpb - 1 - j), st.tpc - 1 - j), b, 0, 0)

    row = lambda i: (i, 0)
    full = lambda shape: pl.BlockSpec(shape, lambda i: (0,) * len(shape))
    o_a, o_f, x_all = _pair(o_a), _pair(o_f), _pair(x_all)
    assert len(o_a) == len(o_f) == len(x_all)
    return pl.pallas_call(
        functools.partial(_mix_out_kernel, n_src=len(x_all), lat_tiles=st.lat_tiles, alpha=alpha),
        grid=(n_tiles,),
        in_specs=_stream_specs(o_a, TM, D_A, st.lat_tiles) + _stream_specs(o_f, TM, D_F, st.lat_tiles)
        + _stream_specs(x_all, TM, D_MODEL, st.lat_tiles)
        + [pl.BlockSpec((1, n, 1, CHUNK, D_C), fwd), pl.BlockSpec((1, n, 1, CHUNK, D_C), bwd),
                  pl.BlockSpec((TM, D_C), row), pl.BlockSpec((TM, D_C), row),
                  pl.BlockSpec((TM, N_BRANCH * D_MODEL), row),
                  pl.BlockSpec((1, 6, D_MODEL), lambda i: (st.mod_index(i), 0, 0)),
                  full((1, D_C)), full((1, D_C)), full((D_C, D_C)),
                  full((D_A, D_MODEL)), full((D_F, D_MODEL)), full((D_C, D_MODEL)), full((D_MODEL, D_MODEL)),
                  full((1, D_MODEL)), full((1, D_MODEL))],
        out_specs=[pl.BlockSpec((TM, D_MODEL), row), pl.BlockSpec((TM, D_MODEL), row)],
        out_shape=[jax.ShapeDtypeStruct((n_tiles * TM, D_MODEL), F32),
                   jax.ShapeDtypeStruct((n_tiles * TM, D_MODEL), BF16)],
        compiler_params=_cparams(("arbitrary",)),
        name="mix_out",
    )(*o_a, *o_f, *x_all, y, y, bonus, rg, gate, mod, lp["ln_w"], lp["ln_b"], lp["head_ones"],
      lp["w_pa"], lp["w_pf"], lp["w_pc"], lp["w_o"], lp["ln1_g"], lp["ln1_b"])


def _ffn_down_kernel(u_ref, g_ref, gp_ref, gn_ref, x_ref, mod_ref, cw_ref, cb_ref, wd_ref, lng_ref, lnb_ref,
                     o_ref, *, st, alpha):
    i = pl.program_id(0)
    first, last = st.seq_pos(i)
    cw = cw_ref[...] * 0.5
    cb = cb_ref[...] * 0.5
    acc = jnp.zeros((TM, D_MODEL), F32)
    for c0 in range(0, D_FF, FF_COLS):
        cs = slice(c0, c0 + FF_COLS)
        g = g_ref[:, cs].astype(F32)
        prev_row = jnp.where(first, 0.0, gp_ref[HALO - 1:HALO, cs].astype(F32))
        next_row = jnp.where(last, 0.0, gn_ref[0:1, cs].astype(F32))
        prev, nxt = _shift_rows(g, prev_row, next_row)
        gh = cw[0:1, cs] * prev + cw[1:2, cs] * g + cw[2:3, cs] * nxt + cb[:, cs]
        act = gh * (1.0 + lax.erf(gh * (2.0 ** 0.5)))
        a = (act * u_ref[:, cs].astype(F32)).astype(BF16)
        acc = acc + _dot(a, wd_ref[cs, :])
    z = alpha * x_ref[...] + mod_ref[0, 5:6, :] * acc
    o_ref[...] = _ln_plain(z) * lng_ref[...] + lnb_ref[...]


def _ffn_down(ug, x1, mod, lp, st, n_tiles, alpha):
    row = lambda i: (i, 0)
    full = lambda shape: pl.BlockSpec(shape, lambda i: (0,) * len(shape))
    gp, gn = _halo_specs(D_FF, 1, ug.shape[0])
    return pl.pallas_call(
        functools.partial(_ffn_down_kernel, st=st, alpha=alpha),
        grid=(n_tiles,),
        in_specs=[pl.BlockSpec((TM, D_FF), lambda i: (i, 0)), pl.BlockSpec((TM, D_FF), lambda i: (i, 1)), gp, gn,
                  pl.BlockSpec((TM, D_MODEL), row),
                  pl.BlockSpec((1, 6, D_MODEL), lambda i: (st.mod_index(i), 0, 0)),
                  full((3, D_FF)), full((1, D_FF)), full((D_FF, D_MODEL)), full((1, D_MODEL)), full((1, D_MODEL))],
        out_specs=pl.BlockSpec((TM, D_MODEL), row),
        out_shape=jax.ShapeDtypeStruct((n_tiles * TM, D_MODEL), F32),
        compiler_params=_cparams(("arbitrary",)),
        name="ffn_down",
    )(ug, ug, ug, ug, x1, mod, lp["ffn_conv"], lp["ffn_conv_b"], lp["ffn_down"], lp["ln2_g"], lp["ln2_b"])


def _rope_tables(st):
    T = st.T
    pos = jnp.arange(T)
    rowp = (pos // GRID_W).astype(F32)
    colp = (pos % GRID_W).astype(F32)
    n_freq = DH_A // 4
    inv = ROPE_BASE ** (-jnp.arange(n_freq, dtype=F32) / n_freq)
    ar, ac = rowp[:, None] * inv, colp[:, None] * inv
    cos64 = jnp.concatenate([jnp.cos(ar), jnp.cos(ar), jnp.cos(ac), jnp.cos(ac)], axis=1)
    sin64 = jnp.concatenate([-jnp.sin(ar), jnp.sin(ar), -jnp.sin(ac), jnp.sin(ac)], axis=1)
    cos_q = jnp.concatenate([cos64, cos64], axis=1)
    sin_q = jnp.concatenate([sin64, sin64], axis=1)
    return cos_q, sin_q


def _block_diag2(m):
    z = jnp.zeros_like(m[0])
    return jnp.concatenate([jnp.concatenate([m[0], z], axis=1), jnp.concatenate([z, m[1]], axis=1)], axis=0)


def _hi_lo(w):
    return jnp.stack(_split2(w))


def kernel(x, c, ctx, c_ctx, ada_w, ada_b, w_in, lam_qk, subln_g, tshift_mu, rw_w0, rw_w2, rw_a0, rw_a2, rw_g2, rw_kk, rw_ka, rw_rk, rw_lnw, rw_lnb, w_pa, w_pf, w_pc, w_o, ln1_g, ln1_b, ffn_up, ffn_conv, ffn_conv_b, ffn_down, ln2_g, ln2_b):
    B, T, D = x.shape
    Lc = ctx.shape[1]
    depth = w_in.shape[0]
    st = _Stream(B, T, Lc)
    alpha = (2.0 * depth) ** 0.25

    cos_q, sin_q = _rope_tables(st)
    dft_lat = _dft_tables(T)
    dft_ctx = _dft_tables(Lc)
    kf = jnp.arange(F_GROUP_W, dtype=jnp.int32)
    ang = ((kf[:, None] * kf[None, :]) % F_GROUP_W).astype(F32) * (2.0 * math.pi / F_GROUP_W)
    dft_ch = jnp.concatenate([jnp.cos(ang), jnp.sin(ang)], axis=1).astype(BF16)
    hi = jnp.arange(D_C) // HS_C
    head_ones = (hi[:, None] == hi[None, :]).astype(BF16)
    c_all = jnp.concatenate([c, c_ctx[None], jnp.zeros((8 - B - 1, D), F32)], axis=0)

    w_qkvf, w_rest = _cast_bf16(w_in, (OFF_CS, D_IN - OFF_CS))
    (w_up,), (w_dn,) = _cast_bf16(ffn_up), _cast_bf16(ffn_down)
    (w_pa_b,), (w_pf_b,), (w_pc_b,), (w_o_b,) = _cast_bf16(w_pa), _cast_bf16(w_pf), _cast_bf16(w_pc), _cast_bf16(w_o)

    x_all = (x.reshape(B * T, D), ctx.reshape(B * Lc, D))
    for i in range(depth):
        last = i == depth - 1
        lam_init = 0.8 - 0.6 * math.exp(-0.3 * i)
        lq = lam_qk[i].astype(F32)
        lam = (jnp.exp(jnp.sum(lq[0] * lq[1])) - jnp.exp(jnp.sum(lq[2] * lq[3])) + lam_init).reshape(1, 1)
        lp = {
            "mu_cs": tshift_mu[i][:, :D_CS], "mu_co": tshift_mu[i][:, D_CS:],
            "w0": rw_w0[i], "w2bd": _hi_lo(_block_diag2(rw_w2[i])), "a0": rw_a0[i],
            "a2bd": _hi_lo(_block_diag2(rw_a2[i])),
            "g2": _hi_lo(rw_g2[i]), "k_k": rw_kk[i].reshape(1, D_C), "k_a": rw_ka[i].reshape(1, D_C),
            "r_k": rw_rk[i].reshape(1, D_C), "ln_w": rw_lnw[i].reshape(1, D_C), "ln_b": rw_lnb[i].reshape(1, D_C),
            "head_ones": head_ones,
            "w_pa": w_pa_b[i], "w_pf": w_pf_b[i], "w_pc": w_pc_b[i], "w_o": w_o_b[i], "ln1_g": ln1_g[i].reshape(1, D), "ln1_b": ln1_b[i].reshape(1, D),
            "ffn_conv": ffn_conv[i], "ffn_conv_b": ffn_conv_b[i].reshape(1, D_FF),
            "ffn_down": w_dn[i], "ln2_g": ln2_g[i].reshape(1, D), "ln2_b": ln2_b[i].reshape(1, D),
        }
        n_out = st.lat_tiles if last else st.tiles

        mod = _adaln(c_all, ada_w, ada_b, i).reshape(8, 6, D)
        q, k, v, z = _proj(x_all, w_qkvf, i, ((D_A, 512, "plain", F32), (D_A, 512, "rope", BF16),
                                                    (D_A, 512, "plain", BF16), (D_F, 512, "dft", BF16)),
                           (cos_q, sin_q, dft_ch), mod, st)
        cs, co, gate = _proj(x_all, w_rest, i, ((D_CS, D_CS // 2, "plain", F32), (D_CO, D_CO, "plain", F32),
                                                       (N_BRANCH * D_MODEL, 512, "sigmoid", BF16)), (), mod, st)

        subln = subln_g[i].reshape(1, DV_A)
        o_a = _attention(lam, q, k, v, cos_q, sin_q, subln, st, 1.0 - lam_init, True)
        o_f = _dft2(dft_lat, z, B, T, 0)
        if not last:
            o_a = (o_a, _attention(lam, q, k, v, cos_q, sin_q, subln, st, 1.0 - lam_init, False))
            o_f = (o_f, _dft2(dft_ctx, z, B, Lc, st.n_lat_rows))
        x_mix = x_all
        if last and isinstance(x_all, tuple):
            x_mix = x_all[0]

        lw, kd, bb, rv, kn, rr, bonus, rg = _rwkv_prep(cs, co, lp, st)
        qh, y0, gm, jm = _rwkv_chunks(lw, kd, bb, rv, kn, rr, st)
        y = _rwkv_carry(qh, y0, gm, jm, st)
        x1, h2 = _mix_out(o_a, o_f, y, bonus, rg, gate, x_mix, mod, lp, st, n_out, alpha)
        (ug,) = _proj(h2, w_up, i, ((2 * D_FF, 512, "plain", BF16),))
        x_all = _ffn_down(ug, x1, mod, lp, st, n_out, alpha)
    return x_all[:B * T].reshape(B, T, D)
```
